```python
import math
import jax, jax.numpy as jnp
from jax import lax
import numpy as np

D_MODEL = 1024
BATCH = 1
SEQ = 16384
DEPTH = 2

GRID_W = 64
CTX_LEN = 256
N_EVEN = (DEPTH + 1) // 2
N_ODD = DEPTH // 2
S5_WIDTH = D_MODEL // 2
S5_H = 16
S5_GROUPS = S5_WIDTH // S5_H
S5_P = 64
CONV_WIDTH = D_MODEL - S5_WIDTH
CONV_K = 31
IN_EVEN = S5_WIDTH + 2 * CONV_WIDTH
POOL_WINDOWS = (2, 4, 8, 16)
POOL_GROUPS = len(POOL_WINDOWS)
POOL_CH = D_MODEL // POOL_GROUPS
D_FF = 4 * D_MODEL
N_MOD = 6
EPS = 1e-6
DT_MIN = 1e-3
DT_MAX = 1e-1
LAMBDA_RE_MAX = -1e-4

kernel_name = 'hybrid_s5_conformer_pool_dit'


def rms_norm(x, g):
    xf = x.astype(jnp.float32)
    y = xf * lax.rsqrt(jnp.mean(xf * xf, axis=-1, keepdims=True) + EPS)
    return (y * g.astype(jnp.float32)).astype(x.dtype)


def layer_norm(x, g, b):
    xf = x.astype(jnp.float32)
    xc = xf - jnp.mean(xf, axis=-1, keepdims=True)
    var = jnp.mean(xc * xc, axis=-1, keepdims=True)
    y = xc * lax.rsqrt(var + EPS) * g.astype(jnp.float32) + b.astype(jnp.float32)
    return y.astype(x.dtype)


def modulate(h, shift, scale):
    return h * (1 + scale[:, None]) + shift[:, None]


def ada_mod(cvec, w, b, dtype):
    m = jax.nn.silu(cvec.astype(jnp.float32)) @ w.astype(jnp.float32) + b.astype(jnp.float32)
    return jnp.split(m.astype(dtype), N_MOD, axis=-1)


def grid_pos_embed(rows, dtype):
    row = jnp.repeat(jnp.arange(rows, dtype=jnp.float32), GRID_W)
    col = jnp.tile(jnp.arange(GRID_W, dtype=jnp.float32), rows)
    quarter = D_MODEL // 4
    omega = 1.0 / (10000.0 ** (jnp.arange(quarter, dtype=jnp.float32) / quarter))
    def emb(pos):
        ang = pos[:, None] * omega[None, :]
        return jnp.concatenate([jnp.sin(ang), jnp.cos(ang)], axis=-1)
    return jnp.concatenate([emb(row), emb(col)], axis=-1).astype(dtype)


def s5_discretize(lam_re, lam_im, log_step, b_re, b_im, c_re, c_im):
    lam = lax.complex(jnp.minimum(lam_re.astype(jnp.float32), LAMBDA_RE_MAX), lam_im.astype(jnp.float32))
    dt = jnp.exp(log_step.astype(jnp.float32))[:, None]
    lam_bar = jnp.exp(lam * dt)
    b = lax.complex(b_re.astype(jnp.float32), b_im.astype(jnp.float32))
    b_bar = ((lam_bar - 1.0) / lam)[..., None] * b
    cm = lax.complex(c_re.astype(jnp.float32), c_im.astype(jnp.float32))
    return lam_bar, b_bar, cm


def _ssm_combine(left, right):
    a_l, b_l = left
    a_r, b_r = right
    return a_r * a_l, a_r * b_l + b_r


def s5_scan(ug, lam_bar, b_bar, cm, s0):
    bu = jnp.einsum('blgh,gph->blgp', ug.astype(jnp.complex64), b_bar)
    bu = bu.at[:, 0].add(lam_bar[None] * s0)
    a = jnp.broadcast_to(lam_bar, bu.shape)
    _, states = lax.associative_scan(_ssm_combine, (a, bu), axis=1)
    y = jnp.einsum('blgp,ghp->blgh', states, cm).real
    return y, states[:, -1]


def even_mixer(h, ep, s0_f, s0_b):
    (w_in, w_out, lam_re, lam_im, log_step, b_re, b_im, c_re, c_im,
     d_skip, w_glu, conv_w, conv_b, ln_g, ln_b) = ep
    bsz, length, _ = h.shape
    z = h @ w_in
    u = z[..., :S5_WIDTH]
    v = z[..., S5_WIDTH:S5_WIDTH + CONV_WIDTH]
    gt = z[..., S5_WIDTH + CONV_WIDTH:]
    uf = u.astype(jnp.float32)
    ug = uf.reshape(bsz, length, S5_GROUPS, S5_H)
    lb_f, bb_f, cm_f = s5_discretize(lam_re[0], lam_im[0], log_step[0], b_re[0], b_im[0], c_re[0], c_im[0])
    lb_b, bb_b, cm_b = s5_discretize(lam_re[1], lam_im[1], log_step[1], b_re[1], b_im[1], c_re[1], c_im[1])
    y_f, s_f = s5_scan(ug, lb_f, bb_f, cm_f, s0_f)
    y_b, s_b = s5_scan(jnp.flip(ug, axis=1), lb_b, bb_b, cm_b, s0_b)
    ya = (y_f + jnp.flip(y_b, axis=1)).reshape(bsz, length, S5_WIDTH) + d_skip.astype(jnp.float32) * uf
    ya = jax.nn.gelu(ya)
    ya = ya * jax.nn.sigmoid(ya @ w_glu.astype(jnp.float32))
    vb = v * jax.nn.sigmoid(gt)
    vb = lax.conv_general_dilated(vb, conv_w.astype(vb.dtype)[:, None, :], window_strides=(1,),
                                  padding=[(CONV_K // 2, CONV_K // 2)],
                                  dimension_numbers=('NWC', 'WIO', 'NWC'),
                                  feature_group_count=CONV_WIDTH) + conv_b
    vb = jax.nn.silu(layer_norm(vb, ln_g, ln_b))
    y = jnp.concatenate([ya.astype(h.dtype), vb], axis=-1) @ w_out
    return y, s_f, s_b


def centred_mean_minus_self(xg, win):
    bsz, length, ch = xg.shape
    left = win // 2
    right = win - 1 - left
    cs = jnp.concatenate([jnp.zeros((bsz, 1, ch), xg.dtype), jnp.cumsum(xg, axis=1)], axis=1)
    t = jnp.arange(length)
    lo = jnp.maximum(t - left, 0)
    hi = jnp.minimum(t + right, length - 1)
    s = jnp.take(cs, hi + 1, axis=1) - jnp.take(cs, lo, axis=1)
    cnt = (hi - lo + 1).astype(jnp.float32)[None, :, None]
    return s / cnt - xg


def pool_mixer(h, pool_w, pool_b, pool_scale):
    hf = h.astype(jnp.float32)
    outs = []
    for gi, win in enumerate(POOL_WINDOWS):
        hg = hf[..., gi * POOL_CH:(gi + 1) * POOL_CH]
        pg = centred_mean_minus_self(hg, win)
        outs.append(pg @ pool_w[gi].astype(jnp.float32) + pool_b[gi].astype(jnp.float32))
    y = jnp.concatenate(outs, axis=-1) * pool_scale.astype(jnp.float32)
    return y.astype(h.dtype)


def sq_relu_mlp(h, w1, w2):
    return jnp.square(jax.nn.relu(h @ w1)) @ w2


def setup_inputs(seed: int = 0) -> dict:
    key = jax.random.key(seed)
    ks = jax.random.split(key, 32)
    f32 = jnp.float32
    def nrm(k, shape, s):
        return jax.random.normal(k, shape, f32) * s
    n_idx = jnp.arange(S5_P, dtype=f32)
    return {
        'x': nrm(ks[0], (BATCH, SEQ, D_MODEL), 1.0),
        'c': nrm(ks[1], (BATCH, D_MODEL), 1.0),
        'ctx': nrm(ks[2], (BATCH, CTX_LEN, D_MODEL), 1.0),
        'c_ctx': nrm(ks[3], (D_MODEL,), 1.0),
        'w_ada': nrm(ks[4], (DEPTH, D_MODEL, N_MOD * D_MODEL), 0.5 * D_MODEL ** -0.5),
        'b_ada': nrm(ks[5], (DEPTH, N_MOD * D_MODEL), 0.02),
        'norm_mix_g': 1.0 + nrm(ks[6], (DEPTH, D_MODEL), 0.02),
        'norm_mlp_g': 1.0 + nrm(ks[7], (DEPTH, D_MODEL), 0.02),
        'w_in': nrm(ks[8], (N_EVEN, D_MODEL, IN_EVEN), D_MODEL ** -0.5),
        'w_out': nrm(ks[9], (N_EVEN, D_MODEL, D_MODEL), D_MODEL ** -0.5),
        's5_lam_re': -0.5 + nrm(ks[10], (N_EVEN, 2, S5_GROUPS, S5_P), 0.01),
        's5_lam_im': math.pi * n_idx + nrm(ks[11], (N_EVEN, 2, S5_GROUPS, S5_P), 0.01),
        's5_log_step': jax.random.uniform(ks[12], (N_EVEN, 2, S5_GROUPS), f32,
                                          minval=math.log(DT_MIN), maxval=math.log(DT_MAX)),
        's5_b_re': nrm(ks[13], (N_EVEN, 2, S5_GROUPS, S5_P, S5_H), (2 * S5_H) ** -0.5),
        's5_b_im': nrm(ks[14], (N_EVEN, 2, S5_GROUPS, S5_P, S5_H), (2 * S5_H) ** -0.5),
        's5_c_re': nrm(ks[15], (N_EVEN, 2, S5_GROUPS, S5_H, S5_P), S5_P ** -0.5),
        's5_c_im': nrm(ks[16], (N_EVEN, 2, S5_GROUPS, S5_H, S5_P), S5_P ** -0.5),
        's5_d': nrm(ks[17], (N_EVEN, S5_WIDTH), 0.5),
        's5_w_glu': nrm(ks[18], (N_EVEN, S5_WIDTH, S5_WIDTH), S5_WIDTH ** -0.5),
        'conv_w': nrm(ks[19], (N_EVEN, CONV_K, CONV_WIDTH), CONV_K ** -0.5),
        'conv_b': nrm(ks[20], (N_EVEN, CONV_WIDTH), 0.02),
        'conv_ln_g': 1.0 + nrm(ks[21], (N_EVEN, CONV_WIDTH), 0.02),
        'conv_ln_b': nrm(ks[22], (N_EVEN, CONV_WIDTH), 0.02),
        'pool_w': nrm(ks[23], (N_ODD, POOL_GROUPS, POOL_CH, POOL_CH), POOL_CH ** -0.5),
        'pool_b': nrm(ks[24], (N_ODD, POOL_GROUPS, POOL_CH), 0.02),
        'pool_scale': 1.0 + nrm(ks[25], (N_ODD, D_MODEL), 0.1),
        'mlp_w1': nrm(ks[26], (DEPTH, D_MODEL, D_FF), D_MODEL ** -0.5),
        'mlp_w2': nrm(ks[27], (DEPTH, D_FF, D_MODEL), D_FF ** -0.5),
        'final_g': 1.0 + nrm(ks[28], (D_MODEL,), 0.02),
    }


def reference(x, c, ctx, c_ctx, w_ada, b_ada, norm_mix_g, norm_mlp_g, w_in, w_out,
              s5_lam_re, s5_lam_im, s5_log_step, s5_b_re, s5_b_im, s5_c_re, s5_c_im,
              s5_d, s5_w_glu, conv_w, conv_b, conv_ln_g, conv_ln_b,
              pool_w, pool_b, pool_scale, mlp_w1, mlp_w2, final_g):
    bsz, length, _ = x.shape
    ROWS = length // GRID_W
    h = x + grid_pos_embed(ROWS, x.dtype)[None]
    hc = ctx
    for i in range(DEPTH):
        j = i // 2
        ctx_next = i < DEPTH - 1
        sh1, sc1, g1, sh2, sc2, g2 = ada_mod(c, w_ada[i], b_ada[i], h.dtype)
        n_x = modulate(rms_norm(h, norm_mix_g[i]), sh1, sc1)
        if i % 2 == 0 or ctx_next:
            csh1, csc1, cg1, csh2, csc2, cg2 = ada_mod(c_ctx[None], w_ada[i], b_ada[i], hc.dtype)
            n_c = modulate(rms_norm(hc, norm_mix_g[i]), csh1, csc1)
        if i % 2 == 0:
            ep = (w_in[j], w_out[j], s5_lam_re[j], s5_lam_im[j], s5_log_step[j],
                  s5_b_re[j], s5_b_im[j], s5_c_re[j], s5_c_im[j], s5_d[j], s5_w_glu[j],
                  conv_w[j], conv_b[j], conv_ln_g[j], conv_ln_b[j])
            zeros = jnp.zeros((hc.shape[0], S5_GROUPS, S5_P), jnp.complex64)
            y_c, s_f, s_b = even_mixer(n_c, ep, zeros, zeros)
            y_x, _, _ = even_mixer(n_x, ep, s_f, s_b)
        else:
            y_x = pool_mixer(n_x, pool_w[j], pool_b[j], pool_scale[j])
            if ctx_next:
                y_c = pool_mixer(n_c, pool_w[j], pool_b[j], pool_scale[j])
        h = h + g1[:, None] * y_x
        h = h + g2[:, None] * sq_relu_mlp(modulate(rms_norm(h, norm_mlp_g[i]), sh2, sc2), mlp_w1[i], mlp_w2[i])
        if ctx_next:
            hc = hc + cg1[:, None] * y_c
            hc = hc + cg2[:, None] * sq_relu_mlp(modulate(rms_norm(hc, norm_mlp_g[i]), csh2, csc2), mlp_w1[i], mlp_w2[i])
    return rms_norm(h, final_g)
```

```python
import functools

import jax
import jax.numpy as jnp
from jax import lax
from jax.experimental import pallas as pl
from jax.experimental.pallas import tpu as pltpu

GRID_W = 64
S5_H = 16
S5_P = 64
CONV_K = 31
POOL_WINDOWS = (2, 4, 8, 16)
N_MOD = 6
EPS = 1e-6
LAMBDA_RE_MAX = -1e-4

LANES = 128
SUBLANES = 8
T_CHUNK = 16
SEG = SUBLANES
GROUPS_PER_LANEBLOCK = LANES // S5_H
VMEM_LIMIT = 56 * 1024 * 1024

F32 = jnp.float32
BF16 = jnp.bfloat16


def _dot(a, b):
    return jnp.dot(a, b, preferred_element_type=F32)


def _cparams(sem):
    return pltpu.CompilerParams(dimension_semantics=sem, vmem_limit_bytes=VMEM_LIMIT)


def _ada_kernel(cc_ref, w_ref, b_ref, o_ref):
    cc = cc_ref[...]
    s = cc * jax.nn.sigmoid(cc)
    s_hi = s.astype(BF16)
    s_lo = (s - s_hi.astype(F32)).astype(BF16)
    w = w_ref[...]
    w_hi = w.astype(BF16)
    w_lo = (w - w_hi.astype(F32)).astype(BF16)
    o_ref[...] = _dot(s_hi, w_hi) + _dot(s_lo, w_hi) + _dot(s_hi, w_lo) + b_ref[...]


def _ada(cc, w_ada, b_ada):
    depth, d, n = w_ada.shape
    tn = 1536
    rows = cc.shape[0]
    return pl.pallas_call(
        _ada_kernel,
        grid=(depth, n // tn),
        in_specs=[
            pl.BlockSpec((rows, d), lambda l, j: (0, 0)),
            pl.BlockSpec((None, d, tn), lambda l, j: (l, 0, j)),
            pl.BlockSpec((None, 1, tn), lambda l, j: (l, 0, j)),
        ],
        out_specs=pl.BlockSpec((None, rows, tn), lambda l, j: (l, 0, j)),
        out_shape=jax.ShapeDtypeStruct((depth, rows, n), F32),
        compiler_params=_cparams(("arbitrary", "arbitrary")),
        name="ada_mod",
    )(cc, w_ada, b_ada.reshape(depth, 1, n))


def _slot_ids(nrows):
    return lax.broadcasted_iota(jnp.int32, (nrows, LANES), 1) // S5_H


def _to_chunked(zs_refs, nchunk, t_chunk):
    slot = _slot_ids(nchunk)
    width = len(zs_refs) * LANES
    halves = t_chunk // GROUPS_PER_LANEBLOCK
    outs = {}
    for j, zs_ref in enumerate(zs_refs):
        for m in range(halves):
            pieces = [zs_ref[pl.ds(m * 8 + tl, nchunk, stride=t_chunk), :] for tl in range(8)]
            for i in range(8):
                acc = None
                for tl in range(8):
                    sh = ((tl - i) % 8) * S5_H
                    r = pieces[tl] if sh == 0 else pltpu.roll(pieces[tl], sh, 1)
                    acc = r if acc is None else jnp.where(slot == tl, r, acc)
                outs[(8 * j + i, m)] = acc
    ngroups = width // S5_H
    return [jnp.concatenate([outs[(g, m)] for m in range(halves)], axis=1) for g in range(ngroups)]


def _from_chunked(y2_ref, ys_refs, nchunk, t_chunk):
    slot = _slot_ids(nchunk)
    halves = t_chunk // GROUPS_PER_LANEBLOCK
    for j, ys_ref in enumerate(ys_refs):
        for m in range(halves):
            srcs = [y2_ref[8 * j + i, :, m * LANES:(m + 1) * LANES] for i in range(8)]
            for tl in range(8):
                acc = None
                for i in range(8):
                    sh = ((i - tl) % 8) * S5_H
                    r = srcs[i] if sh == 0 else pltpu.roll(srcs[i], sh, 1)
                    acc = r if acc is None else jnp.where(slot == i, r, acc)
                ys_ref[pl.ds(m * 8 + tl, nchunk, stride=t_chunk), :] = acc


def _add_pos(xs, rt_ref, ct_ref, r):
    half = ct_ref.shape[1]
    pos = jnp.concatenate([jnp.broadcast_to(rt_ref[r:r + 1, :], (GRID_W, half)), ct_ref[...]], axis=1)
    return xs + pos


def _rms_mod(h, gm, sh):
    ms = jnp.mean(h * h, axis=-1, keepdims=True)
    return h * lax.rsqrt(ms + EPS) * gm + sh


def _inproj_kernel(*refs, tm, use_pos, t_chunk, s5w):
    if use_pos:
        x_ref, rt_ref, ct_ref, g_ref, sh_ref, sc_ref, w_ref, u2_ref, vb_ref, n_s, *z_s = refs
    else:
        x_ref, g_ref, sh_ref, sc_ref, w_ref, u2_ref, vb_ref, n_s, *z_s = refs
    gm = g_ref[...] * (1.0 + sc_ref[...])
    sh = sh_ref[...]
    for r in range(tm // GRID_W):
        rows = slice(r * GRID_W, (r + 1) * GRID_W)
        xs = x_ref[rows, :]
        if use_pos:
            xs = _add_pos(xs, rt_ref, ct_ref, r)
        n_s[rows, :] = _rms_mod(xs, gm, sh).astype(BF16)
    z = _dot(n_s[...], w_ref[...])
    for j, zs_ref in enumerate(z_s):
        zs_ref[...] = z[:, j * LANES:(j + 1) * LANES]
    cw = (z.shape[1] - s5w) // 2
    vb_ref[...] = z[:, s5w:s5w + cw] * jax.nn.sigmoid(z[:, s5w + cw:])
    chunks = _to_chunked(z_s, tm // t_chunk, t_chunk)
    for g, ch in enumerate(chunks):
        u2_ref[g] = ch.astype(BF16)


def _inproj(x, tabs, g, sh, sc, w_in, *, tm, t_chunk, s5w, cps):
    rows, d = x.shape
    n_in = w_in.shape[1]
    ngroups = s5w // S5_H
    kc = t_chunk * S5_H
    cpt = tm // t_chunk
    use_pos = tabs is not None
    vec = pl.BlockSpec((1, d), lambda i: (0, 0))
    in_specs = [pl.BlockSpec((tm, d), lambda i: (i, 0))]
    args = [x]
    if use_pos:
        rt, ct = tabs
        in_specs += [pl.BlockSpec((tm // GRID_W, rt.shape[1]), lambda i: (i, 0)),
                     pl.BlockSpec(ct.shape, lambda i: (0, 0))]
        args += [rt, ct]
    in_specs += [vec, vec, vec, pl.BlockSpec((d, n_in), lambda i: (0, 0))]
    args += [g, sh, sc, w_in]
    if rows // t_chunk > cps:
        bps = cps // cpt
        u2_shape = (ngroups, cps, SEG * kc)
        u2_spec = pl.BlockSpec((ngroups, cpt, kc), lambda i: (0, i % bps, i // bps))
    else:
        u2_shape = (ngroups, rows // t_chunk, kc)
        u2_spec = pl.BlockSpec((ngroups, cpt, kc), lambda i: (0, i, 0))
    cw = (n_in - s5w) // 2
    return pl.pallas_call(
        functools.partial(_inproj_kernel, tm=tm, use_pos=use_pos, t_chunk=t_chunk, s5w=s5w),
        grid=(rows // tm,),
        in_specs=in_specs,
        out_specs=[u2_spec, pl.BlockSpec((tm, cw), lambda i: (i, 0))],
        out_shape=[jax.ShapeDtypeStruct(u2_shape, BF16), jax.ShapeDtypeStruct((rows, cw), F32)],
        scratch_shapes=[pltpu.VMEM((tm, d), BF16)] + [pltpu.VMEM((tm, LANES), F32)] * (s5w // LANES),
        compiler_params=_cparams(("arbitrary",)),
        name="in_proj",
    )(*args)


def _cmul_add(ar, ai, sr, si, zr, zi):
    return ar * sr - ai * si + zr, ar * si + ai * sr + zi


def _s5core_kernel(u2_ref, u2c_ref, wsum_ref, mfb_ref, cmat_ref, a_ref, y2_ref, z_s, s_s, car_s, *, gb, cps, ncc):
    half = LANES // 2
    isf8 = lax.broadcasted_iota(jnp.int32, (SUBLANES, LANES), 1) < half
    isf1 = lax.broadcasted_iota(jnp.int32, (1, LANES), 1) < half

    for q in range(gb):
        z_s[q] = _dot(u2_ref[q], wsum_ref[q])

    ctx_fin = []
    for q in range(gb):
        zc = _dot(u2c_ref[q], wsum_ref[q])
        ar, ai = a_ref[q, 0:1, :], a_ref[q, 1:2, :]
        sr = jnp.zeros((1, LANES), F32)
        si = jnp.zeros((1, LANES), F32)
        for k in range(ncc):
            kb = ncc - 1 - k
            zr = jnp.where(isf1, zc[k:k + 1, 0:LANES], zc[kb:kb + 1, 0:LANES])
            zi = jnp.where(isf1, zc[k:k + 1, LANES:], zc[kb:kb + 1, LANES:])
            sr, si = _cmul_add(ar, ai, sr, si, zr, zi)
        ctx_fin.append((sr, si))

    def body1(k, carry):
        rf = pl.multiple_of(k * SUBLANES, SUBLANES)
        rb = pl.multiple_of((cps - 1 - k) * SUBLANES, SUBLANES)
        new = []
        for q in range(gb):
            sr, si = carry[q]
            s_s[q, pl.ds(rf, SUBLANES), 0:half] = sr[:, 0:half]
            s_s[q, pl.ds(rb, SUBLANES), half:LANES] = sr[:, half:]
            s_s[q, pl.ds(rf, SUBLANES), LANES:LANES + half] = si[:, 0:half]
            s_s[q, pl.ds(rb, SUBLANES), LANES + half:] = si[:, half:]
            zr = jnp.where(isf8, z_s[q, pl.ds(rf, SUBLANES), 0:LANES], z_s[q, pl.ds(rb, SUBLANES), 0:LANES])
            zi = jnp.where(isf8, z_s[q, pl.ds(rf, SUBLANES), LANES:], z_s[q, pl.ds(rb, SUBLANES), LANES:])
            new.append(_cmul_add(a_ref[q, 0:1, :], a_ref[q, 1:2, :], sr, si, zr, zi))
        return tuple(new)

    zero8 = jnp.zeros((SUBLANES, LANES), F32)
    fin = lax.fori_loop(0, cps, body1, tuple((zero8, zero8) for _ in range(gb)))

    for q in range(gb):
        fr, fi = fin[q]
        asr, asi = a_ref[q, 2:3, :], a_ref[q, 3:4, :]
        cr, ci = ctx_fin[q]
        car_s[q, 0, 0:1, 0:half] = cr[:, 0:half]
        car_s[q, 0, SEG - 1:SEG, half:] = cr[:, half:]
        car_s[q, 1, 0:1, 0:half] = ci[:, 0:half]
        car_s[q, 1, SEG - 1:SEG, half:] = ci[:, half:]
        for m in range(SEG - 1):
            mb = SEG - 1 - m
            zr = jnp.where(isf1, fr[m:m + 1, :], fr[mb:mb + 1, :])
            zi = jnp.where(isf1, fi[m:m + 1, :], fi[mb:mb + 1, :])
            cr, ci = _cmul_add(asr, asi, cr, ci, zr, zi)
            car_s[q, 0, m + 1:m + 2, 0:half] = cr[:, 0:half]
            car_s[q, 0, mb - 1:mb, half:] = cr[:, half:]
            car_s[q, 1, m + 1:m + 2, 0:half] = ci[:, 0:half]
            car_s[q, 1, mb - 1:mb, half:] = ci[:, half:]

    def body2(k, pw):
        rf = pl.multiple_of(k * SUBLANES, SUBLANES)
        rb = pl.multiple_of((cps - 1 - k) * SUBLANES, SUBLANES)
        new = []
        for q in range(gb):
            pr, pi = pw[q]
            cr, ci = car_s[q, 0], car_s[q, 1]
            tr = pr * cr - pi * ci
            ti = pr * ci + pi * cr
            s_s[q, pl.ds(rf, SUBLANES), 0:half] += tr[:, 0:half]
            s_s[q, pl.ds(rb, SUBLANES), half:LANES] += tr[:, half:]
            s_s[q, pl.ds(rf, SUBLANES), LANES:LANES + half] += ti[:, 0:half]
            s_s[q, pl.ds(rb, SUBLANES), LANES + half:] += ti[:, half:]
            ar, ai = a_ref[q, 0:1, :], a_ref[q, 1:2, :]
            new.append((pr * ar - pi * ai, pr * ai + pi * ar))
        return tuple(new)

    one8 = jnp.ones((SUBLANES, LANES), F32)
    lax.fori_loop(0, cps, body2, tuple((one8, zero8) for _ in range(gb)))

    for q in range(gb):
        y2_ref[q] = _dot(u2_ref[q], mfb_ref[q]) + _dot(s_s[q].astype(BF16), cmat_ref[q])


def _s5core(u2, u2c, wsum, mfb, cmat, avec, *, gb, cps):
    ngroups, nrows, kc = u2.shape
    ncc = u2c.shape[1]
    ns = wsum.shape[2]
    blk = lambda *shape: pl.BlockSpec((gb,) + shape, lambda i: (i,) + (0,) * len(shape))
    return pl.pallas_call(
        functools.partial(_s5core_kernel, gb=gb, cps=cps, ncc=ncc),
        grid=(ngroups // gb,),
        in_specs=[blk(nrows, kc), blk(ncc, kc), blk(kc, ns), blk(kc, kc), blk(ns, kc), blk(SUBLANES, LANES)],
        out_specs=blk(nrows, kc),
        out_shape=jax.ShapeDtypeStruct((ngroups, nrows, kc), F32),
        scratch_shapes=[pltpu.VMEM((gb, nrows, ns), F32), pltpu.VMEM((gb, nrows, ns), F32),
                        pltpu.VMEM((gb, 2, SUBLANES, LANES), F32)],
        compiler_params=_cparams(("arbitrary",)),
        name="s5_core",
    )(u2, u2c, wsum, mfb, cmat, avec)


CONV_HALO = 16


def _outproj_kernel(x_ref, rt_ref, ct_ref, y2_ref, vb_ref, vbp_ref, vbn_ref, wglu_ref, cw_ref, cb_ref,
                    lng_ref, lnb_ref, wout_ref, g1_ref, o_ref, ext_s, cat_s, *ys_s, tm, t_chunk):
    i = pl.program_id(0)
    last = pl.num_programs(0) - 1
    s5w = len(ys_s) * LANES
    _from_chunked(y2_ref, ys_s, tm // t_chunk, t_chunk)

    rb = 128
    for r in range(tm // rb):
        rows = slice(r * rb, (r + 1) * rb)
        ya = jax.nn.gelu(jnp.concatenate([ys_ref[rows, :] for ys_ref in ys_s], axis=1))
        ya = ya * jax.nn.sigmoid(_dot(ya.astype(BF16), wglu_ref[...]))
        cat_s[rows, 0:s5w] = ya.astype(BF16)

    ext_s[0:CONV_HALO, :] = jnp.where(i > 0, vbp_ref[...], 0.0)
    ext_s[CONV_HALO:CONV_HALO + tm, :] = vb_ref[...]
    ext_s[CONV_HALO + tm:, :] = jnp.where(i < last, vbn_ref[...], 0.0)
    pad = CONV_K // 2
    cb = 64
    for r in range(tm // cb):
        acc = jnp.broadcast_to(cb_ref[...], (cb, cb_ref.shape[1]))
        for k in range(CONV_K):
            off = r * cb + CONV_HALO - pad + k
            acc = acc + cw_ref[k:k + 1, :] * ext_s[off:off + cb, :]
        mu = jnp.mean(acc, axis=-1, keepdims=True)
        xc = acc - mu
        var = jnp.mean(xc * xc, axis=-1, keepdims=True)
        y = xc * lax.rsqrt(var + EPS) * lng_ref[...] + lnb_ref[...]
        y = y * jax.nn.sigmoid(y)
        cat_s[r * cb:(r + 1) * cb, s5w:] = y.astype(BF16)

    yx = _dot(cat_s[...], wout_ref[...])
    g1 = g1_ref[...]
    for r in range(tm // GRID_W):
        rows = slice(r * GRID_W, (r + 1) * GRID_W)
        h = _add_pos(x_ref[rows, :], rt_ref, ct_ref, r)
        o_ref[rows, :] = h + g1 * yx[rows, :]


def _outproj(x, tabs, y2, vb, wglu, conv_w, conv_b, ln_g, ln_b, wout, g1, *, tm, t_chunk, cps):
    rows, d = x.shape
    ngroups, _, kc = y2.shape
    s5w = ngroups * S5_H
    cw = vb.shape[1]
    cpt = tm // t_chunk
    bps = cps // cpt
    rt, ct = tabs
    y2v = y2.reshape(ngroups, cps, SEG * kc)
    hb = tm // CONV_HALO
    nhb = rows // CONV_HALO
    full = lambda a: pl.BlockSpec(a.shape, lambda i: (0,) * a.ndim)
    return pl.pallas_call(
        functools.partial(_outproj_kernel, tm=tm, t_chunk=t_chunk),
        grid=(rows // tm,),
        in_specs=[
            pl.BlockSpec((tm, d), lambda i: (i, 0)),
            pl.BlockSpec((tm // GRID_W, rt.shape[1]), lambda i: (i, 0)),
            full(ct),
            pl.BlockSpec((ngroups, cpt, kc), lambda i: (0, i % bps, i // bps)),
            pl.BlockSpec((tm, cw), lambda i: (i, 0)),
            pl.BlockSpec((CONV_HALO, cw), lambda i: (jnp.maximum(i * hb - 1, 0), 0)),
            pl.BlockSpec((CONV_HALO, cw), lambda i: (jnp.minimum((i + 1) * hb, nhb - 1), 0)),
            full(wglu), full(conv_w), full(conv_b), full(ln_g), full(ln_b), full(wout), full(g1),
        ],
        out_specs=pl.BlockSpec((tm, d), lambda i: (i, 0)),
        out_shape=jax.ShapeDtypeStruct((rows, d), F32),
        scratch_shapes=[pltpu.VMEM((tm + 2 * CONV_HALO, cw), F32), pltpu.VMEM((tm, d), BF16)]
        + [pltpu.VMEM((tm, LANES), F32)] * (s5w // LANES),
        compiler_params=_cparams(("arbitrary",)),
        name="out_proj",
    )(x, rt, ct, y2v, vb, vb, vb, wglu, conv_w, conv_b, ln_g, ln_b, wout, g1)


def _mlp_kernel(h_ref, g_ref, sh_ref, sc_ref, g2_ref, w1_ref, w2_ref, fg_ref, o_ref, *, ff_chunk, final):
    h = h_ref[...]
    gm = g_ref[...] * (1.0 + sc_ref[...])
    nb = _rms_mod(h, gm, sh_ref[...]).astype(BF16)
    acc = None
    for c in range(w1_ref.shape[1] // ff_chunk):
        cols = slice(c * ff_chunk, (c + 1) * ff_chunk)
        a = jnp.maximum(_dot(nb, w1_ref[:, cols]), 0.0)
        p = _dot((a * a).astype(BF16), w2_ref[cols, :])
        acc = p if acc is None else acc + p
    out = h + g2_ref[...] * acc
    if final:
        ms = jnp.mean(out * out, axis=-1, keepdims=True)
        out = out * lax.rsqrt(ms + EPS) * fg_ref[...]
    o_ref[...] = out


def _mlp(h, g, sh, sc, g2, w1, w2, fg, *, tm, final):
    rows, d = h.shape
    vec = pl.BlockSpec((1, d), lambda i: (0, 0))
    full = lambda a: pl.BlockSpec(a.shape, lambda i: (0,) * a.ndim, pipeline_mode=pl.Buffered(1))
    return pl.pallas_call(
        functools.partial(_mlp_kernel, ff_chunk=1024, final=final),
        grid=(rows // tm,),
        in_specs=[pl.BlockSpec((tm, d), lambda i: (i, 0)), vec, vec, vec, vec, full(w1), full(w2), vec],
        out_specs=pl.BlockSpec((tm, d), lambda i: (i, 0)),
        out_shape=jax.ShapeDtypeStruct((rows, d), F32),
        compiler_params=_cparams(("arbitrary",)),
        name="mlp_final" if final else "mlp",
    )(h, g, sh, sc, g2, w1, w2, fg)


POOL_HALO = 8


def _pool_kernel(h_ref, hp_ref, hn_ref, g_ref, sh_ref, sc_ref, g1_ref, pw_ref, pb_ref, ps_ref, o_ref, ext_s,
                 *, tm, seq_len):
    i = pl.program_id(0)
    last = pl.num_programs(0) - 1
    gm = g_ref[...] * (1.0 + sc_ref[...])
    sh = sh_ref[...]
    ext_s[0:POOL_HALO, :] = jnp.where(i > 0, _rms_mod(hp_ref[...], gm, sh), 0.0)
    ext_s[POOL_HALO:POOL_HALO + tm, :] = _rms_mod(h_ref[...], gm, sh)
    ext_s[POOL_HALO + tm:, :] = jnp.where(i < last, _rms_mod(hn_ref[...], gm, sh), 0.0)
    t = i * tm + lax.broadcasted_iota(jnp.int32, (tm, 1), 0)
    pc = pw_ref.shape[1]
    for gi, win in enumerate(POOL_WINDOWS):
        cols = slice(gi * pc, (gi + 1) * pc)
        left = win // 2
        right = win - 1 - left
        s = None
        for off in range(-left, right + 1):
            v = ext_s[POOL_HALO + off:POOL_HALO + off + tm, cols]
            s = v if s is None else s + v
        lo = jnp.maximum(t - left, 0)
        hi = jnp.minimum(t + right, seq_len - 1)
        cnt = (hi - lo + 1).astype(F32)
        pg = s / cnt - ext_s[POOL_HALO:POOL_HALO + tm, cols]
        y = (_dot(pg.astype(BF16), pw_ref[gi]) + pb_ref[:, cols]) * ps_ref[:, cols]
        o_ref[:, cols] = h_ref[:, cols] + g1_ref[:, cols] * y


def _pool(h, g, sh, sc, g1, pool_w, pool_b, pool_scale, *, tm):
    rows, d = h.shape
    hb = tm // POOL_HALO
    nhb = rows // POOL_HALO
    vec = pl.BlockSpec((1, d), lambda i: (0, 0))
    return pl.pallas_call(
        functools.partial(_pool_kernel, tm=tm, seq_len=rows),
        grid=(rows // tm,),
        in_specs=[
            pl.BlockSpec((tm, d), lambda i: (i, 0)),
            pl.BlockSpec((POOL_HALO, d), lambda i: (jnp.maximum(i * hb - 1, 0), 0)),
            pl.BlockSpec((POOL_HALO, d), lambda i: (jnp.minimum((i + 1) * hb, nhb - 1), 0)),
            vec, vec, vec, vec,
            pl.BlockSpec(pool_w.shape, lambda i: (0, 0, 0)),
            vec, vec,
        ],
        out_specs=pl.BlockSpec((tm, d), lambda i: (i, 0)),
        out_shape=jax.ShapeDtypeStruct((rows, d), F32),
        scratch_shapes=[pltpu.VMEM((tm + 2 * POOL_HALO, d), F32)],
        compiler_params=_cparams(("arbitrary",)),
        name="pool_mixer",
    )(h, h, h, g, sh, sc, g1, pool_w, pool_b, pool_scale)


def _pos_tables(rows, d):
    quarter = d // 4
    omega = 1.0 / (10000.0 ** (jnp.arange(quarter, dtype=F32) / quarter))

    def emb(n):
        ang = jnp.arange(n, dtype=F32)[:, None] * omega[None, :]
        return jnp.concatenate([jnp.sin(ang), jnp.cos(ang)], axis=-1)

    return emb(rows), emb(GRID_W)


def _s5_matrices(lam_re, lam_im, log_step, b_re, b_im, c_re, c_im, d_skip, t_chunk, cps):
    hp = lax.Precision.HIGHEST
    ngroups, npole = lam_re.shape[1:]
    lam = lax.complex(jnp.minimum(lam_re.astype(F32), LAMBDA_RE_MAX), lam_im.astype(F32))
    lam_dt = lam * jnp.exp(log_step.astype(F32))[..., None]
    lam_bar = jnp.exp(lam_dt)
    bb = ((lam_bar - 1.0) / lam)[..., None] * lax.complex(b_re.astype(F32), b_im.astype(F32))
    cm = lax.complex(c_re.astype(F32), c_im.astype(F32))
    pw = jnp.exp(lam_dt[..., None] * jnp.arange(t_chunk + 1, dtype=F32))
    pw_k = jnp.moveaxis(pw, -1, 2)

    kern = jnp.real(jnp.einsum("dgkhp,dgpj->dgkhj", cm[:, :, None] * pw_k[:, :, :t_chunk, None, :], bb,
                               precision=hp))
    tau = jnp.arange(t_chunk)
    lag = tau[None, :] - tau[:, None]
    kc = t_chunk * S5_H

    def toeplitz(kd, lg):
        blocks = jnp.where((lg >= 0)[None, :, :, None, None], kd[:, jnp.clip(lg, 0)], 0.0)
        return jnp.transpose(blocks, (0, 1, 4, 2, 3)).reshape(ngroups, kc, kc)

    skip = (jnp.eye(t_chunk, dtype=F32)[None, :, None, :, None]
            * jnp.eye(S5_H, dtype=F32)[None, None, :, None, :]
            * d_skip.astype(F32).reshape(ngroups, 1, S5_H, 1, 1)).reshape(ngroups, kc, kc)
    mfb = toeplitz(kern[0], lag) + toeplitz(kern[1], -lag) + skip

    def summary(d, expo):
        return pw_k[d][:, expo, None, :] * jnp.swapaxes(bb[d], 1, 2)[:, None, :, :]

    bs_f = summary(0, t_chunk - 1 - tau).reshape(ngroups, kc, npole)
    bs_b = summary(1, tau).reshape(ngroups, kc, npole)
    wsum = jnp.concatenate([bs_f.real, bs_b.real, bs_f.imag, bs_b.imag], axis=-1)

    def readout(d, expo):
        return jnp.swapaxes(cm[d], 1, 2)[:, :, None, :] * jnp.swapaxes(pw_k[d][:, expo, :], 1, 2)[:, :, :, None]

    e_f = readout(0, tau + 1).reshape(ngroups, npole, kc)
    e_b = readout(1, t_chunk - tau).reshape(ngroups, npole, kc)
    cmat = jnp.concatenate([e_f.real, e_b.real, -e_f.imag, -e_b.imag], axis=1)

    a_t = pw[..., t_chunk]
    a_s = jnp.exp(lam_dt * float(t_chunk * cps))
    cat2 = lambda v: jnp.concatenate([v[0], v[1]], axis=-1)
    avec = jnp.stack([cat2(a_t.real), cat2(a_t.imag), cat2(a_s.real), cat2(a_s.imag)], axis=1)
    avec = jnp.concatenate([avec, jnp.zeros((ngroups, SUBLANES - 4, 2 * npole), F32)], axis=1)
    return wsum.astype(BF16), mfb.astype(BF16), cmat.astype(BF16), avec


def kernel(x, c, ctx, c_ctx, w_ada, b_ada, norm_mix_g, norm_mlp_g, w_in, w_out, s5_lam_re, s5_lam_im, s5_log_step, s5_b_re, s5_b_im, s5_c_re, s5_c_im, s5_d, s5_w_glu, conv_w, conv_b, conv_ln_g, conv_ln_b, pool_w, pool_b, pool_scale, mlp_w1, mlp_w2, final_g):
    bsz, seq, d = x.shape
    assert bsz == 1 and c.shape[0] == 1 and ctx.shape[0] == 1
    assert w_ada.shape[0] == 2, "the kernels implement the depth-2 block (one even layer, one odd layer)"
    s5w = s5_d.shape[1]
    nch = seq // T_CHUNK
    cps = nch // SEG
    tm = 512

    xs = x[0]
    tabs = _pos_tables(seq // GRID_W, d)

    cc = jnp.zeros((16, d), F32).at[0].set(c[0].astype(F32)).at[1].set(c_ctx.astype(F32))
    mods = _ada(cc, w_ada, b_ada)

    def mod(layer, row, which):
        return lax.slice(mods[layer], (row, which * d), (row + 1, (which + 1) * d))

    row_vec = lambda v: v.reshape(1, -1).astype(F32)

    wsum, mfb, cmat, avec = _s5_matrices(s5_lam_re[0], s5_lam_im[0], s5_log_step[0], s5_b_re[0], s5_b_im[0],
                                         s5_c_re[0], s5_c_im[0], s5_d[0], T_CHUNK, cps)
    w_in_b = w_in[0].astype(BF16)
    g_mix0 = row_vec(norm_mix_g[0])
    u2, vb = _inproj(xs, tabs, g_mix0, mod(0, 0, 0), mod(0, 0, 1), w_in_b,
                     tm=tm, t_chunk=T_CHUNK, s5w=s5w, cps=cps)
    u2c, _ = _inproj(ctx[0], None, g_mix0, mod(0, 1, 0), mod(0, 1, 1), w_in_b,
                     tm=ctx.shape[1], t_chunk=T_CHUNK, s5w=s5w, cps=cps)
    y2 = _s5core(u2.reshape(u2.shape[0], nch, T_CHUNK * S5_H), u2c, wsum, mfb, cmat, avec, gb=4, cps=cps)
    conv_w_p = jnp.concatenate([conv_w[0].astype(F32), jnp.zeros((1, conv_w.shape[2]), F32)], axis=0)
    h = _outproj(xs, tabs, y2, vb, s5_w_glu[0].astype(BF16), conv_w_p, row_vec(conv_b[0]),
                 row_vec(conv_ln_g[0]), row_vec(conv_ln_b[0]), w_out[0].astype(BF16), mod(0, 0, 2),
                 tm=tm, t_chunk=T_CHUNK, cps=cps)
    fg = row_vec(final_g)
    h = _mlp(h, row_vec(norm_mlp_g[0]), mod(0, 0, 3), mod(0, 0, 4), mod(0, 0, 5),
             mlp_w1[0].astype(BF16), mlp_w2[0].astype(BF16), fg, tm=tm, final=False)

    h = _pool(h, row_vec(norm_mix_g[1]), mod(1, 0, 0), mod(1, 0, 1), mod(1, 0, 2),
              pool_w[0].astype(BF16), row_vec(pool_b[0]), row_vec(pool_scale[0]), tm=tm)
    h = _mlp(h, row_vec(norm_mlp_g[1]), mod(1, 0, 3), mod(1, 0, 4), mod(1, 0, 5),
             mlp_w1[1].astype(BF16), mlp_w2[1].astype(BF16), fg, tm=tm, final=True)
    return h[None]
```

```python
import functools

import jax
import jax.numpy as jnp
from jax import lax
from jax.experimental import pallas as pl
from jax.experimental.pallas import tpu as pltpu

GRID_W = 64
S5_H = 16
CONV_K = 31
POOL_WINDOWS = (2, 4, 8, 16)
EPS = 1e-6
LAMBDA_RE_MAX = -1e-4

LANES = 128
SUBLANES = 8
T_CHUNK = 16
SEG = SUBLANES
SLAB = GRID_W
GROUPS_PER_LANEBLOCK = LANES // S5_H
VMEM_LIMIT = 60 * 1024 * 1024

F32 = jnp.float32
BF16 = jnp.bfloat16


def _dot(a, b):
    return jnp.dot(a, b, preferred_element_type=F32)


def _dot_nt(a, b):
    return lax.dot_general(a, b, (((1,), (1,)), ((), ())), preferred_element_type=F32)


def _cparams(sem):
    return pltpu.CompilerParams(dimension_semantics=sem, vmem_limit_bytes=VMEM_LIMIT)


def _resident(a):
    return pl.BlockSpec(a.shape, lambda *_: (0,) * a.ndim, pipeline_mode=pl.Buffered(1))


def _ada_kernel(cc_ref, w_ref, b_ref, o_ref):
    cc = cc_ref[...]
    s = cc * jax.nn.sigmoid(cc)
    s_hi = s.astype(BF16)
    s_lo = (s - s_hi.astype(F32)).astype(BF16)
    w = w_ref[...]
    w_hi = w.astype(BF16)
    w_lo = (w - w_hi.astype(F32)).astype(BF16)
    o_ref[...] = _dot(s_hi, w_hi) + _dot(s_lo, w_hi) + _dot(s_hi, w_lo) + b_ref[...]


def _ada(cc, w_ada, b_ada):
    depth, d, n = w_ada.shape
    tn = 1536
    rows = cc.shape[0]
    return pl.pallas_call(
        _ada_kernel,
        grid=(depth, n // tn),
        in_specs=[
            pl.BlockSpec((rows, d), lambda l, j: (0, 0)),
            pl.BlockSpec((None, d, tn), lambda l, j: (l, 0, j)),
            pl.BlockSpec((None, 1, tn), lambda l, j: (l, 0, j)),
        ],
        out_specs=pl.BlockSpec((None, rows, tn), lambda l, j: (l, 0, j)),
        out_shape=jax.ShapeDtypeStruct((depth, rows, n), F32),
        compiler_params=_cparams(("arbitrary", "arbitrary")),
        name="ada_mod",
    )(cc, w_ada, b_ada.reshape(depth, 1, n))


def _slot_ids(nrows):
    return lax.broadcasted_iota(jnp.int32, (nrows, LANES), 1) // S5_H


def _to_chunked(zs_refs, nblk, nchunk, stride):
    slot = _slot_ids(nchunk)
    halves = T_CHUNK // GROUPS_PER_LANEBLOCK
    outs = {}
    for j, zs_ref in enumerate(zs_refs):
        for b in range(nblk):
            for m in range(halves):
                pieces = [zs_ref[pl.ds(b * T_CHUNK + m * 8 + tl, nchunk, stride=stride), :] for tl in range(8)]
                for i in range(8):
                    acc = None
                    for tl in range(8):
                        sh = ((tl - i) % 8) * S5_H
                        r = pieces[tl] if sh == 0 else pltpu.roll(pieces[tl], sh, 1)
                        acc = r if acc is None else jnp.where(slot == tl, r, acc)
                    outs[(8 * j + i, b, m)] = acc
    res = []
    for g in range(len(zs_refs) * GROUPS_PER_LANEBLOCK):
        rows = [jnp.concatenate([outs[(g, b, m)] for m in range(halves)], axis=1) for b in range(nblk)]
        res.append(rows[0] if nblk == 1 else jnp.concatenate(rows, axis=0))
    return res


def _from_chunked(y2_ref, ys_refs, nblk, nchunk, stride):
    slot = _slot_ids(nchunk)
    halves = T_CHUNK // GROUPS_PER_LANEBLOCK
    for j, ys_ref in enumerate(ys_refs):
        for b in range(nblk):
            for m in range(halves):
                srcs = [y2_ref[8 * j + i, b * nchunk:(b + 1) * nchunk, m * LANES:(m + 1) * LANES] for i in range(8)]
                for tl in range(8):
                    acc = None
                    for i in range(8):
                        sh = ((i - tl) % 8) * S5_H
                        r = srcs[i] if sh == 0 else pltpu.roll(srcs[i], sh, 1)
                        acc = r if acc is None else jnp.where(slot == i, r, acc)
                    ys_ref[pl.ds(b * T_CHUNK + m * 8 + tl, nchunk, stride=stride), :] = acc


def _add_pos(xs, rt_ref, ct_ref, seg):
    half = ct_ref.shape[1]
    pos = jnp.concatenate([jnp.broadcast_to(rt_ref[seg:seg + 1, :], (GRID_W, half)), ct_ref[...]], axis=1)
    return xs + pos


def _rms_mod(h, gm, sh):
    ms = jnp.mean(h * h, axis=-1, keepdims=True)
    return h * lax.rsqrt(ms + EPS) * gm + sh


def _inproj_kernel(*refs, latent, s5w):
    if latent:
        x_ref, rt_ref, ct_ref, g_ref, sh_ref, sc_ref, w_ref, u2_ref, vb_ref, n_s, *z_s = refs
    else:
        x_ref, g_ref, sh_ref, sc_ref, w_ref, u2_ref, n_s, *z_s = refs
    nseg, rseg, _ = x_ref.shape
    gm = g_ref[...] * (1.0 + sc_ref[...])
    sh = sh_ref[...]
    for seg in range(nseg):
        for r in range(rseg // GRID_W):
            xs = x_ref[seg, r * GRID_W:(r + 1) * GRID_W, :]
            if latent:
                xs = _add_pos(xs, rt_ref, ct_ref, seg)
            row0 = seg * rseg + r * GRID_W
            n_s[row0:row0 + GRID_W, :] = _rms_mod(xs, gm, sh).astype(BF16)
    z = _dot(n_s[...], w_ref[...].astype(BF16))
    for j, zs_ref in enumerate(z_s):
        zs_ref[...] = z[:, j * LANES:(j + 1) * LANES]
    if latent:
        cw = (z.shape[1] - s5w) // 2
        vb = z[:, s5w:s5w + cw] * jax.nn.sigmoid(z[:, s5w + cw:])
        for seg in range(nseg):
            vb_ref[seg] = vb[seg * rseg:(seg + 1) * rseg, :]
        chunks = _to_chunked(z_s, rseg // T_CHUNK, nseg, rseg)
    else:
        chunks = _to_chunked(z_s, 1, rseg // T_CHUNK, T_CHUNK)
    for g, ch in enumerate(chunks):
        u2_ref[g] = ch.astype(BF16)


def _inproj(x3, tabs, g, sh, sc, w_in, *, s5w):
    nseg, rps, d = x3.shape
    n_in = w_in.shape[1]
    ngroups = s5w // S5_H
    kc = T_CHUNK * S5_H
    latent = tabs is not None
    rseg = SLAB if latent else rps
    tm = nseg * rseg
    cpt = tm // T_CHUNK
    vec = pl.BlockSpec((1, d), lambda i: (0, 0))
    in_specs = [pl.BlockSpec((nseg, rseg, d), lambda i: (0, i, 0))]
    args = [x3]
    if latent:
        rt, ct = tabs
        in_specs += [pl.BlockSpec((None, nseg, rt.shape[2]), lambda i: (i, 0, 0)), _resident(ct)]
        args += [rt, ct]
    in_specs += [vec, vec, vec, _resident(w_in)]
    args += [g, sh, sc, w_in]
    out_specs = [pl.BlockSpec((ngroups, cpt, kc), lambda i: (0, i, 0))]
    out_shape = [jax.ShapeDtypeStruct((ngroups, nseg * rps // T_CHUNK, kc), BF16)]
    if latent:
        cw = (n_in - s5w) // 2
        out_specs.append(pl.BlockSpec((nseg, rseg, cw), lambda i: (0, i, 0)))
        out_shape.append(jax.ShapeDtypeStruct((nseg, rps, cw), F32))
    return pl.pallas_call(
        functools.partial(_inproj_kernel, latent=latent, s5w=s5w),
        grid=(rps // rseg,),
        in_specs=in_specs,
        out_specs=out_specs,
        out_shape=out_shape,
        scratch_shapes=[pltpu.VMEM((tm, d), BF16)] + [pltpu.VMEM((tm, LANES), F32)] * (s5w // LANES),
        compiler_params=_cparams(("arbitrary",)),
        name="in_proj" if latent else "in_proj_ctx",
    )(*args)


def _cmul_add(ar, ai, sr, si, zr, zi):
    return ar * sr - ai * si + zr, ar * si + ai * sr + zi


def _shift_lanes(lo, hi, s):
    lane = lax.broadcasted_iota(jnp.int32, lo.shape, 1)
    zero = jnp.zeros_like(lo)
    if s == 0:
        return lo, hi
    if s < 0:
        hi2, lo2 = _shift_lanes_right_of_reversed(hi, lo, -s, lane, zero)
        return lo2, hi2
    return _shift_right(lo, hi, s, lane, zero)


def _shift_right(lo, hi, s, lane, zero):
    if s >= LANES:
        r = s - LANES
        return zero, (lo if r == 0 else jnp.where(lane >= r, pltpu.roll(lo, r, 1), zero))
    rl, rh = pltpu.roll(lo, s, 1), pltpu.roll(hi, s, 1)
    return jnp.where(lane >= s, rl, zero), jnp.where(lane >= s, rh, rl)


def _shift_lanes_right_of_reversed(first, second, s, lane, zero):
    if s >= LANES:
        r = s - LANES
        return zero, (first if r == 0 else jnp.where(lane < LANES - r, pltpu.roll(first, LANES - r, 1), zero))
    rf, rs = pltpu.roll(first, LANES - s, 1), pltpu.roll(second, LANES - s, 1)
    keep = lane < LANES - s
    return jnp.where(keep, rf, zero), jnp.where(keep, rs, rf)


def _s5core_kernel(u2_ref, u2c_ref, wsum_ref, kt_ref, cmat_ref, a_ref, y2_ref, z_s, s_s, car_s, mfb_s,
                   *, gb, cps, ncc):
    half = LANES // 2
    isf8 = lax.broadcasted_iota(jnp.int32, (SUBLANES, LANES), 1) < half
    isf1 = lax.broadcasted_iota(jnp.int32, (1, LANES), 1) < half

    sub = lax.broadcasted_iota(jnp.int32, (S5_H, LANES), 0)
    lane = lax.broadcasted_iota(jnp.int32, (S5_H, LANES), 1)
    for q in range(gb):
        kf_lo, kf_hi = kt_ref[q, 0:S5_H, 0:LANES], kt_ref[q, 0:S5_H, LANES:]
        kb_lo, kb_hi = kt_ref[q, S5_H:2 * S5_H, 0:LANES], kt_ref[q, S5_H:2 * S5_H, LANES:]
        dsk = kt_ref[q, 2 * S5_H:2 * S5_H + 1, 0:LANES]
        for ti in range(T_CHUNK):
            f_lo, f_hi = _shift_lanes(kf_lo, kf_hi, ti * S5_H)
            b_lo, b_hi = _shift_lanes(kb_lo, kb_hi, -(T_CHUNK - 1 - ti) * S5_H)
            diag = jnp.where(lane == (ti % 8) * S5_H + sub, dsk, 0.0)
            lo = f_lo + b_lo + (diag if ti < 8 else 0.0)
            hi = f_hi + b_hi + (diag if ti >= 8 else 0.0)
            mfb_s[q, ti * S5_H:(ti + 1) * S5_H, :] = jnp.concatenate([lo, hi], axis=1).astype(BF16)

    for q in range(gb):
        z_s[q] = _dot_nt(u2_ref[q], wsum_ref[q])

    ctx_fin = []
    for q in range(gb):
        zc = _dot_nt(u2c_ref[q], wsum_ref[q])
        ar, ai = a_ref[q, 0:1, :], a_ref[q, 1:2, :]
        sr = jnp.zeros((1, LANES), F32)
        si = jnp.zeros((1, LANES), F32)
        for k in range(ncc):
            kb = ncc - 1 - k
            zr = jnp.where(isf1, zc[k:k + 1, 0:LANES], zc[kb:kb + 1, 0:LANES])
            zi = jnp.where(isf1, zc[k:k + 1, LANES:], zc[kb:kb + 1, LANES:])
            sr, si = _cmul_add(ar, ai, sr, si, zr, zi)
        ctx_fin.append((sr, si))

    def body1(k, carry):
        rf = pl.multiple_of(k * SUBLANES, SUBLANES)
        rb = pl.multiple_of((cps - 1 - k) * SUBLANES, SUBLANES)
        new = []
        for q in range(gb):
            sr, si = carry[q]
            s_s[q, pl.ds(rf, SUBLANES), 0:half] = sr[:, 0:half]
            s_s[q, pl.ds(rb, SUBLANES), half:LANES] = sr[:, half:]
            s_s[q, pl.ds(rf, SUBLANES), LANES:LANES + half] = si[:, 0:half]
            s_s[q, pl.ds(rb, SUBLANES), LANES + half:] = si[:, half:]
            zr = jnp.where(isf8, z_s[q, pl.ds(rf, SUBLANES), 0:LANES], z_s[q, pl.ds(rb, SUBLANES), 0:LANES])
            zi = jnp.where(isf8, z_s[q, pl.ds(rf, SUBLANES), LANES:], z_s[q, pl.ds(rb, SUBLANES), LANES:])
            new.append(_cmul_add(a_ref[q, 0:1, :], a_ref[q, 1:2, :], sr, si, zr, zi))
        return tuple(new)

    zero8 = jnp.zeros((SUBLANES, LANES), F32)
    fin = lax.fori_loop(0, cps, body1, tuple((zero8, zero8) for _ in range(gb)))

    for q in range(gb):
        fr, fi = fin[q]
        asr, asi = a_ref[q, 2:3, :], a_ref[q, 3:4, :]
        cr, ci = ctx_fin[q]
        car_s[q, 0, 0:1, 0:half] = cr[:, 0:half]
        car_s[q, 0, SEG - 1:SEG, half:] = cr[:, half:]
        car_s[q, 1, 0:1, 0:half] = ci[:, 0:half]
        car_s[q, 1, SEG - 1:SEG, half:] = ci[:, half:]
        for m in range(SEG - 1):
            mb = SEG - 1 - m
            zr = jnp.where(isf1, fr[m:m + 1, :], fr[mb:mb + 1, :])
            zi = jnp.where(isf1, fi[m:m + 1, :], fi[mb:mb + 1, :])
            cr, ci = _cmul_add(asr, asi, cr, ci, zr, zi)
            car_s[q, 0, m + 1:m + 2, 0:half] = cr[:, 0:half]
            car_s[q, 0, mb - 1:mb, half:] = cr[:, half:]
            car_s[q, 1, m + 1:m + 2, 0:half] = ci[:, 0:half]
            car_s[q, 1, mb - 1:mb, half:] = ci[:, half:]

    def body2(k, pw):
        rf = pl.multiple_of(k * SUBLANES, SUBLANES)
        rb = pl.multiple_of((cps - 1 - k) * SUBLANES, SUBLANES)
        new = []
        for q in range(gb):
            pr, pi = pw[q]
            cr, ci = car_s[q, 0], car_s[q, 1]
            tr = pr * cr - pi * ci
            ti = pr * ci + pi * cr
            s_s[q, pl.ds(rf, SUBLANES), 0:half] += tr[:, 0:half]
            s_s[q, pl.ds(rb, SUBLANES), half:LANES] += tr[:, half:]
            s_s[q, pl.ds(rf, SUBLANES), LANES:LANES + half] += ti[:, 0:half]
            s_s[q, pl.ds(rb, SUBLANES), LANES + half:] += ti[:, half:]
            ar, ai = a_ref[q, 0:1, :], a_ref[q, 1:2, :]
            new.append((pr * ar - pi * ai, pr * ai + pi * ar))
        return tuple(new)

    one8 = jnp.ones((SUBLANES, LANES), F32)
    lax.fori_loop(0, cps, body2, tuple((one8, zero8) for _ in range(gb)))

    for q in range(gb):
        y2_ref[q] = _dot(u2_ref[q], mfb_s[q]) + _dot(s_s[q].astype(BF16), cmat_ref[q])


def _s5core(u2, u2c, wsum_t, kt, cmat, avec, *, gb, cps):
    ngroups, nrows, kc = u2.shape
    ncc = u2c.shape[1]
    ns = wsum_t.shape[1]
    blk = lambda a: pl.BlockSpec((gb,) + a.shape[1:], lambda i: (i,) + (0,) * (a.ndim - 1))
    return pl.pallas_call(
        functools.partial(_s5core_kernel, gb=gb, cps=cps, ncc=ncc),
        grid=(ngroups // gb,),
        in_specs=[blk(u2), blk(u2c), blk(wsum_t), blk(kt), blk(cmat), blk(avec)],
        out_specs=pl.BlockSpec((gb, nrows, kc), lambda i: (i, 0, 0)),
        out_shape=jax.ShapeDtypeStruct((ngroups, nrows, kc), F32),
        scratch_shapes=[pltpu.VMEM((gb, nrows, ns), F32), pltpu.VMEM((gb, nrows, ns), F32),
                        pltpu.VMEM((gb, 2, SUBLANES, LANES), F32), pltpu.VMEM((gb, kc, kc), BF16)],
        compiler_params=_cparams(("arbitrary",)),
        name="s5_core",
    )(u2, u2c, wsum_t, kt, cmat, avec)


CONV_HALO = 16


def _outproj_kernel(x_ref, rt_ref, ct_ref, y2_ref, vb_ref, vbp_ref, vbn_ref, wglu_ref, cw_ref, cb_ref,
                    lng_ref, lnb_ref, wout_ref, g1_ref, o_ref, ext_s, cat_s, *ys_s):
    i = pl.program_id(0)
    last = pl.num_programs(0) - 1
    nseg, rseg, _ = x_ref.shape
    tm = nseg * rseg
    s5w = len(ys_s) * LANES
    cwid = vb_ref.shape[2]
    _from_chunked(y2_ref, ys_s, rseg // T_CHUNK, nseg, rseg)

    wglu = wglu_ref[...].astype(BF16)
    rb = 128
    for r in range(tm // rb):
        rows = slice(r * rb, (r + 1) * rb)
        ya = jax.nn.gelu(jnp.concatenate([ys_ref[rows, :] for ys_ref in ys_s], axis=1))
        ya = ya * jax.nn.sigmoid(_dot(ya.astype(BF16), wglu))
        cat_s[rows, 0:s5w] = ya.astype(BF16)

    zeros_h = jnp.zeros((CONV_HALO, cwid), F32)
    pad = CONV_K // 2
    ext_rows = rseg + 2 * CONV_HALO
    nacc = rseg + SUBLANES
    for seg in range(nseg):
        prev_first = vbp_ref[seg - 1] if seg > 0 else zeros_h
        next_last = vbn_ref[seg + 1] if seg < nseg - 1 else zeros_h
        ext_s[0:CONV_HALO, :] = jnp.where(i > 0, vbp_ref[seg], prev_first)
        ext_s[CONV_HALO:CONV_HALO + rseg, :] = vb_ref[seg]
        ext_s[CONV_HALO + rseg:ext_rows, :] = jnp.where(i < last, vbn_ref[seg], next_last)
        acc = jnp.broadcast_to(cb_ref[...], (rseg, cwid))
        for s in range(SUBLANES):
            part = None
            for qq in range(ext_rows // SUBLANES):
                k = qq * SUBLANES + s - (CONV_HALO - pad)
                if k < 0 or k >= CONV_K or qq * SUBLANES + nacc > ext_rows:
                    continue
                term = cw_ref[k:k + 1, :] * ext_s[qq * SUBLANES:qq * SUBLANES + nacc, :]
                part = term if part is None else part + term
            acc = acc + part[s:s + rseg, :]
        mu = jnp.mean(acc, axis=-1, keepdims=True)
        xc = acc - mu
        var = jnp.mean(xc * xc, axis=-1, keepdims=True)
        y = xc * lax.rsqrt(var + EPS) * lng_ref[...] + lnb_ref[...]
        y = y * jax.nn.sigmoid(y)
        cat_s[seg * rseg:(seg + 1) * rseg, s5w:] = y.astype(BF16)

    yx = _dot(cat_s[...], wout_ref[...].astype(BF16))
    g1 = g1_ref[...]
    for seg in range(nseg):
        h = _add_pos(x_ref[seg], rt_ref, ct_ref, seg)
        o_ref[seg] = h + g1 * yx[seg * rseg:(seg + 1) * rseg, :]


def _outproj(x3, tabs, y2, vb3, wglu, conv_w, conv_b, ln_g, ln_b, wout, g1):
    nseg, rps, d = x3.shape
    ngroups, _, kc = y2.shape
    s5w = ngroups * S5_H
    cw = vb3.shape[2]
    rseg = SLAB
    tm = nseg * rseg
    rt, ct = tabs
    hb = rseg // CONV_HALO
    nhb = rps // CONV_HALO
    vec = lambda a: pl.BlockSpec(a.shape, lambda i: (0,) * a.ndim)
    tile = lambda w: pl.BlockSpec((nseg, rseg, w), lambda i: (0, i, 0))
    return pl.pallas_call(
        _outproj_kernel,
        grid=(rps // rseg,),
        in_specs=[
            tile(d),
            pl.BlockSpec((None, nseg, rt.shape[2]), lambda i: (i, 0, 0)),
            _resident(ct),
            pl.BlockSpec((ngroups, tm // T_CHUNK, kc), lambda i: (0, i, 0)),
            tile(cw),
            pl.BlockSpec((nseg, CONV_HALO, cw), lambda i: (0, (i * hb + nhb - 1) % nhb, 0)),
            pl.BlockSpec((nseg, CONV_HALO, cw), lambda i: (0, ((i + 1) * hb) % nhb, 0)),
            _resident(wglu), vec(conv_w), vec(conv_b), vec(ln_g), vec(ln_b), _resident(wout), vec(g1),
        ],
        out_specs=tile(d),
        out_shape=jax.ShapeDtypeStruct((nseg, rps, d), F32),
        scratch_shapes=[pltpu.VMEM((rseg + 2 * CONV_HALO, cw), F32), pltpu.VMEM((tm, d), BF16)]
        + [pltpu.VMEM((tm, LANES), F32)] * (s5w // LANES),
        compiler_params=_cparams(("arbitrary",)),
        name="out_proj",
    )(x3, rt, ct, y2, vb3, vb3, vb3, wglu, conv_w, conv_b, ln_g, ln_b, wout, g1)


def _mlp_kernel(h_ref, g_ref, sh_ref, sc_ref, g2_ref, w1_ref, w2_ref, fg_ref, o_ref, *, ff_chunk, final):
    h = h_ref[...]
    gm = g_ref[...] * (1.0 + sc_ref[...])
    nb = _rms_mod(h, gm, sh_ref[...]).astype(BF16)
    acc = None
    for c in range(w1_ref.shape[1] // ff_chunk):
        cols = slice(c * ff_chunk, (c + 1) * ff_chunk)
        a = jnp.maximum(_dot(nb, w1_ref[:, cols].astype(BF16)), 0.0)
        p = _dot((a * a).astype(BF16), w2_ref[cols, :].astype(BF16))
        acc = p if acc is None else acc + p
    out = h + g2_ref[...] * acc
    if final:
        ms = jnp.mean(out * out, axis=-1, keepdims=True)
        out = out * lax.rsqrt(ms + EPS) * fg_ref[...]
    o_ref[...] = out


def _mlp(h, g, sh, sc, g2, w1, w2, fg, *, tm, final):
    rows, d = h.shape
    vec = pl.BlockSpec((1, d), lambda i: (0, 0))
    return pl.pallas_call(
        functools.partial(_mlp_kernel, ff_chunk=1024, final=final),
        grid=(rows // tm,),
        in_specs=[pl.BlockSpec((tm, d), lambda i: (i, 0)), vec, vec, vec, vec, _resident(w1), _resident(w2), vec],
        out_specs=pl.BlockSpec((tm, d), lambda i: (i, 0)),
        out_shape=jax.ShapeDtypeStruct((rows, d), F32),
        compiler_params=_cparams(("arbitrary",)),
        name="mlp_final" if final else "mlp",
    )(h, g, sh, sc, g2, w1, w2, fg)


POOL_HALO = 8


def _pool_kernel(h_ref, hp_ref, hn_ref, g_ref, sh_ref, sc_ref, g1_ref, pw_ref, pb_ref, ps_ref, o_ref, ext_s,
                 *, tm, seq_len):
    i = pl.program_id(0)
    last = pl.num_programs(0) - 1
    gm = g_ref[...] * (1.0 + sc_ref[...])
    sh = sh_ref[...]
    ext_s[0:POOL_HALO, :] = jnp.where(i > 0, _rms_mod(hp_ref[...], gm, sh), 0.0)
    ext_s[POOL_HALO:POOL_HALO + tm, :] = _rms_mod(h_ref[...], gm, sh)
    ext_s[POOL_HALO + tm:, :] = jnp.where(i < last, _rms_mod(hn_ref[...], gm, sh), 0.0)
    t = i * tm + lax.broadcasted_iota(jnp.int32, (tm, 1), 0)
    pc = pw_ref.shape[1]
    for gi, win in enumerate(POOL_WINDOWS):
        cols = slice(gi * pc, (gi + 1) * pc)
        left = win // 2
        right = win - 1 - left
        s = None
        for off in range(-left, right + 1):
            v = ext_s[POOL_HALO + off:POOL_HALO + off + tm, cols]
            s = v if s is None else s + v
        lo = jnp.maximum(t - left, 0)
        hi = jnp.minimum(t + right, seq_len - 1)
        cnt = (hi - lo + 1).astype(F32)
        pg = s / cnt - ext_s[POOL_HALO:POOL_HALO + tm, cols]
        y = (_dot(pg.astype(BF16), pw_ref[gi].astype(BF16)) + pb_ref[:, cols]) * ps_ref[:, cols]
        o_ref[:, cols] = h_ref[:, cols] + g1_ref[:, cols] * y


def _pool(h, g, sh, sc, g1, pool_w, pool_b, pool_scale, *, tm):
    rows, d = h.shape
    hb = tm // POOL_HALO
    nhb = rows // POOL_HALO
    vec = pl.BlockSpec((1, d), lambda i: (0, 0))
    return pl.pallas_call(
        functools.partial(_pool_kernel, tm=tm, seq_len=rows),
        grid=(rows // tm,),
        in_specs=[
            pl.BlockSpec((tm, d), lambda i: (i, 0)),
            pl.BlockSpec((POOL_HALO, d), lambda i: (jnp.maximum(i * hb - 1, 0), 0)),
            pl.BlockSpec((POOL_HALO, d), lambda i: (jnp.minimum((i + 1) * hb, nhb - 1), 0)),
            vec, vec, vec, vec, _resident(pool_w), vec, vec,
        ],
        out_specs=pl.BlockSpec((tm, d), lambda i: (i, 0)),
        out_shape=jax.ShapeDtypeStruct((rows, d), F32),
        scratch_shapes=[pltpu.VMEM((tm + 2 * POOL_HALO, d), F32)],
        compiler_params=_cparams(("arbitrary",)),
        name="pool_mixer",
    )(h, h, h, g, sh, sc, g1, pool_w, pool_b, pool_scale)


def _pos_tables(rows, d, nseg):
    quarter = d // 4
    omega = 1.0 / (10000.0 ** (jnp.arange(quarter, dtype=F32) / quarter))

    def emb(n):
        ang = jnp.arange(n, dtype=F32)[:, None] * omega[None, :]
        return jnp.concatenate([jnp.sin(ang), jnp.cos(ang)], axis=-1)

    rt = jnp.swapaxes(emb(rows).reshape(nseg, rows // nseg, 2 * quarter), 0, 1)
    return rt, emb(GRID_W)


def _s5_operators(lam_re, lam_im, log_step, b_re, b_im, c_re, c_im, d_skip, cps):
    hp = lax.Precision.HIGHEST
    ngroups, npole = lam_re.shape[1:]
    t = T_CHUNK
    kc = t * S5_H
    lr = jnp.minimum(lam_re.astype(F32), LAMBDA_RE_MAX)
    li = lam_im.astype(F32)
    dt = jnp.exp(log_step.astype(F32))[..., None]
    ldr, ldi = lr * dt, li * dt

    def cpow(n):
        mag = jnp.exp(ldr * n)
        return mag * jnp.cos(ldi * n), mag * jnp.sin(ldi * n)

    lbr, lbi = cpow(1.0)
    den = lr * lr + li * li
    qr = ((lbr - 1.0) * lr + lbi * li) / den
    qi = (lbi * lr - (lbr - 1.0) * li) / den
    br, bi = b_re.astype(F32), b_im.astype(F32)
    bbr = qr[..., None] * br - qi[..., None] * bi
    bbi = qr[..., None] * bi + qi[..., None] * br
    cr = jnp.swapaxes(c_re.astype(F32), 2, 3)
    ci = jnp.swapaxes(c_im.astype(F32), 2, 3)

    kk = jnp.arange(t + 1, dtype=F32)
    pwr, pwi = cpow(kk[:, None, None, None])
    pwr, pwi = jnp.moveaxis(pwr, 0, -1), jnp.moveaxis(pwi, 0, -1)

    cpr = cr[:, :, :, None, :] * pwr[..., :t, None] - ci[:, :, :, None, :] * pwi[..., :t, None]
    cpi = cr[:, :, :, None, :] * pwi[..., :t, None] + ci[:, :, :, None, :] * pwr[..., :t, None]
    kern = (jnp.einsum("dgpj,dgpkh->dgjkh", bbr, cpr, precision=hp)
            - jnp.einsum("dgpj,dgpkh->dgjkh", bbi, cpi, precision=hp))
    kt_f = kern[0].reshape(ngroups, S5_H, kc)
    kt_b = jnp.flip(kern[1], axis=2).reshape(ngroups, S5_H, kc)
    dsk = jnp.tile(d_skip.astype(F32).reshape(ngroups, 1, S5_H), (1, SUBLANES, t))
    kt = jnp.concatenate([kt_f, kt_b, dsk], axis=1)

    def outer(vr, vi, wr, wi, expo):
        er, ei = vr[..., expo, None], vi[..., expo, None]
        return ((er * wr[:, :, None, :] - ei * wi[:, :, None, :]).reshape(ngroups, npole, kc),
                (er * wi[:, :, None, :] + ei * wr[:, :, None, :]).reshape(ngroups, npole, kc))

    tau = jnp.arange(t)
    sfr, sfi = outer(pwr[0], pwi[0], bbr[0], bbi[0], t - 1 - tau)
    sbr, sbi = outer(pwr[1], pwi[1], bbr[1], bbi[1], tau)
    wsum_t = jnp.concatenate([sfr, sbr, sfi, sbi], axis=1).astype(BF16)

    efr, efi = outer(pwr[0], pwi[0], cr[0], ci[0], tau + 1)
    ebr, ebi = outer(pwr[1], pwi[1], cr[1], ci[1], t - tau)
    cmat = jnp.concatenate([efr, ebr, -efi, -ebi], axis=1).astype(BF16)

    atr, ati = pwr[..., t], pwi[..., t]
    asr, asi = cpow(float(t * cps))
    cat2 = lambda v: jnp.concatenate([v[0], v[1]], axis=-1)
    avec = jnp.stack([cat2(atr), cat2(ati), cat2(asr), cat2(asi)], axis=1)
    avec = jnp.concatenate([avec, jnp.zeros((ngroups, SUBLANES - 4, 2 * npole), F32)], axis=1)
    return wsum_t, kt, cmat, avec


def kernel(x, c, ctx, c_ctx, w_ada, b_ada, norm_mix_g, norm_mlp_g, w_in, w_out, s5_lam_re, s5_lam_im, s5_log_step, s5_b_re, s5_b_im, s5_c_re, s5_c_im, s5_d, s5_w_glu, conv_w, conv_b, conv_ln_g, conv_ln_b, pool_w, pool_b, pool_scale, mlp_w1, mlp_w2, final_g):
    bsz, seq, d = x.shape
    assert bsz == 1 and c.shape[0] == 1 and ctx.shape[0] == 1
    assert w_ada.shape[0] == 2, "the kernels implement the depth-2 block (one even layer, one odd layer)"
    s5w = s5_d.shape[1]
    rps = seq // SEG
    cps = rps // T_CHUNK
    tm = 512

    x3 = x.reshape(SEG, rps, d)
    tabs = _pos_tables(seq // GRID_W, d, SEG)

    cc = jnp.zeros((16, d), F32).at[0].set(c[0].astype(F32)).at[1].set(c_ctx.astype(F32))
    mods = _ada(cc, w_ada, b_ada)

    def mod(layer, row, which):
        return lax.slice(mods[layer], (row, which * d), (row + 1, (which + 1) * d))

    row_vec = lambda v: v.reshape(1, -1).astype(F32)

    wsum_t, kt, cmat, avec = _s5_operators(s5_lam_re[0], s5_lam_im[0], s5_log_step[0], s5_b_re[0], s5_b_im[0],
                                           s5_c_re[0], s5_c_im[0], s5_d[0], cps)
    g_mix0 = row_vec(norm_mix_g[0])
    u2, vb3 = _inproj(x3, tabs, g_mix0, mod(0, 0, 0), mod(0, 0, 1), w_in[0], s5w=s5w)
    u2c, = _inproj(ctx, None, g_mix0, mod(0, 1, 0), mod(0, 1, 1), w_in[0], s5w=s5w)
    y2 = _s5core(u2, u2c, wsum_t, kt, cmat, avec, gb=4, cps=cps)
    h3 = _outproj(x3, tabs, y2, vb3, s5_w_glu[0], conv_w[0].astype(F32), row_vec(conv_b[0]),
                  row_vec(conv_ln_g[0]), row_vec(conv_ln_b[0]), w_out[0], mod(0, 0, 2))
    h = h3.reshape(seq, d)
    fg = row_vec(final_g)
    h = _mlp(h, row_vec(norm_mlp_g[0]), mod(0, 0, 3), mod(0, 0, 4), mod(0, 0, 5),
             mlp_w1[0], mlp_w2[0], fg, tm=tm, final=False)

    h = _pool(h, row_vec(norm_mix_g[1]), mod(1, 0, 0), mod(1, 0, 1), mod(1, 0, 2),
              pool_w[0], row_vec(pool_b[0]), row_vec(pool_scale[0]), tm=tm)
    h = _mlp(h, row_vec(norm_mlp_g[1]), mod(1, 0, 3), mod(1, 0, 4), mod(1, 0, 5),
             mlp_w1[1], mlp_w2[1], fg, tm=tm, final=True)
    return h[None]
```

```python
import functools

import jax
import jax.numpy as jnp
from jax import lax
from jax.experimental import pallas as pl
from jax.experimental.pallas import tpu as pltpu

GRID_W = 64
S5_H = 16
CONV_K = 31
POOL_WINDOWS = (2, 4, 8, 16)
EPS = 1e-6
LAMBDA_RE_MAX = -1e-4

LANES = 128
SUBLANES = 8
T_CHUNK = 16
SEG = SUBLANES
SLAB = GRID_W
GROUPS_PER_LANEBLOCK = LANES // S5_H
VMEM_LIMIT = 60 * 1024 * 1024

F32 = jnp.float32
BF16 = jnp.bfloat16


def _dot(a, b):
    return jnp.dot(a, b, preferred_element_type=F32)


def _dot_nt(a, b):
    return lax.dot_general(a, b, (((1,), (1,)), ((), ())), preferred_element_type=F32)


def _cparams(sem):
    return pltpu.CompilerParams(dimension_semantics=sem, vmem_limit_bytes=VMEM_LIMIT)


def _resident(a):
    return pl.BlockSpec(a.shape, lambda *_: (0,) * a.ndim, pipeline_mode=pl.Buffered(1))


def _ada_kernel(cc_ref, w_ref, b_ref, o_ref):
    cc = cc_ref[...]
    s = cc * jax.nn.sigmoid(cc)
    s_hi = s.astype(BF16)
    s_lo = (s - s_hi.astype(F32)).astype(BF16)
    w = w_ref[...]
    w_hi = w.astype(BF16)
    w_lo = (w - w_hi.astype(F32)).astype(BF16)
    o_ref[...] = _dot(s_hi, w_hi) + _dot(s_lo, w_hi) + _dot(s_hi, w_lo) + b_ref[...]


def _ada(cc, w_ada, b_ada):
    depth, d, n = w_ada.shape
    tn = 1536
    rows = cc.shape[0]
    return pl.pallas_call(
        _ada_kernel,
        grid=(depth, n // tn),
        in_specs=[
            pl.BlockSpec((rows, d), lambda l, j: (0, 0)),
            pl.BlockSpec((None, d, tn), lambda l, j: (l, 0, j)),
            pl.BlockSpec((None, 1, tn), lambda l, j: (l, 0, j)),
        ],
        out_specs=pl.BlockSpec((None, rows, tn), lambda l, j: (l, 0, j)),
        out_shape=jax.ShapeDtypeStruct((depth, rows, n), F32),
        compiler_params=_cparams(("arbitrary", "arbitrary")),
        name="ada_mod",
    )(cc, w_ada, b_ada.reshape(depth, 1, n))


def _slot_ids(nrows):
    return lax.broadcasted_iota(jnp.int32, (nrows, LANES), 1) // S5_H


def _to_chunked(zs_refs, nblk, nchunk, stride):
    slot = _slot_ids(nchunk)
    halves = T_CHUNK // GROUPS_PER_LANEBLOCK
    outs = {}
    for j, zs_ref in enumerate(zs_refs):
        for b in range(nblk):
            for m in range(halves):
                pieces = [zs_ref[pl.ds(b * T_CHUNK + m * 8 + tl, nchunk, stride=stride), :] for tl in range(8)]
                for i in range(8):
                    acc = None
                    for tl in range(8):
                        sh = ((tl - i) % 8) * S5_H
                        r = pieces[tl] if sh == 0 else pltpu.roll(pieces[tl], sh, 1)
                        acc = r if acc is None else jnp.where(slot == tl, r, acc)
                    outs[(8 * j + i, b, m)] = acc
    res = []
    for g in range(len(zs_refs) * GROUPS_PER_LANEBLOCK):
        rows = [jnp.concatenate([outs[(g, b, m)] for m in range(halves)], axis=1) for b in range(nblk)]
        res.append(rows[0] if nblk == 1 else jnp.concatenate(rows, axis=0))
    return res


def _from_chunked(y2_ref, ys_refs, nblk, nchunk, stride):
    slot = _slot_ids(nchunk)
    halves = T_CHUNK // GROUPS_PER_LANEBLOCK
    for j, ys_ref in enumerate(ys_refs):
        for b in range(nblk):
            for m in range(halves):
                srcs = [y2_ref[8 * j + i, b * nchunk:(b + 1) * nchunk, m * LANES:(m + 1) * LANES] for i in range(8)]
                for tl in range(8):
                    acc = None
                    for i in range(8):
                        sh = ((i - tl) % 8) * S5_H
                        r = srcs[i] if sh == 0 else pltpu.roll(srcs[i], sh, 1)
                        acc = r if acc is None else jnp.where(slot == i, r, acc)
                    ys_ref[pl.ds(b * T_CHUNK + m * 8 + tl, nchunk, stride=stride), :] = acc


def _add_pos(xs, rt_ref, ct_ref, seg):
    half = ct_ref.shape[1]
    pos = jnp.concatenate([jnp.broadcast_to(rt_ref[seg:seg + 1, :], (GRID_W, half)), ct_ref[...]], axis=1)
    return xs + pos


def _rms_mod(h, gm, sh):
    ms = jnp.mean(h * h, axis=-1, keepdims=True)
    return h * lax.rsqrt(ms + EPS) * gm + sh


def _inproj_kernel(*refs, latent, s5w):
    if latent:
        x_ref, rt_ref, ct_ref, g_ref, sh_ref, sc_ref, w_ref, u2_ref, vb_ref, n_s, *z_s = refs
    else:
        x_ref, g_ref, sh_ref, sc_ref, w_ref, u2_ref, n_s, *z_s = refs
    nseg, rseg, _ = x_ref.shape
    gm = g_ref[...] * (1.0 + sc_ref[...])
    sh = sh_ref[...]
    for seg in range(nseg):
        for r in range(rseg // GRID_W):
            xs = x_ref[seg, r * GRID_W:(r + 1) * GRID_W, :]
            if latent:
                xs = _add_pos(xs, rt_ref, ct_ref, seg)
            row0 = seg * rseg + r * GRID_W
            n_s[row0:row0 + GRID_W, :] = _rms_mod(xs, gm, sh).astype(BF16)
    z = _dot(n_s[...], w_ref[...].astype(BF16))
    for j, zs_ref in enumerate(z_s):
        zs_ref[...] = z[:, j * LANES:(j + 1) * LANES]
    if latent:
        cw = (z.shape[1] - s5w) // 2
        vb = z[:, s5w:s5w + cw] * jax.nn.sigmoid(z[:, s5w + cw:])
        for seg in range(nseg):
            vb_ref[seg] = vb[seg * rseg:(seg + 1) * rseg, :]
        chunks = _to_chunked(z_s, rseg // T_CHUNK, nseg, rseg)
    else:
        chunks = _to_chunked(z_s, 1, rseg // T_CHUNK, T_CHUNK)
    for g, ch in enumerate(chunks):
        u2_ref[g] = ch.astype(BF16)


def _inproj(x3, tabs, g, sh, sc, w_in, *, s5w):
    nseg, rps, d = x3.shape
    n_in = w_in.shape[1]
    ngroups = s5w // S5_H
    kc = T_CHUNK * S5_H
    latent = tabs is not None
    rseg = SLAB if latent else rps
    tm = nseg * rseg
    cpt = tm // T_CHUNK
    vec = pl.BlockSpec((1, d), lambda i: (0, 0))
    in_specs = [pl.BlockSpec((nseg, rseg, d), lambda i: (0, i, 0))]
    args = [x3]
    if latent:
        rt, ct = tabs
        in_specs += [pl.BlockSpec((None, nseg, rt.shape[2]), lambda i: (i, 0, 0)), _resident(ct)]
        args += [rt, ct]
    in_specs += [vec, vec, vec, _resident(w_in)]
    args += [g, sh, sc, w_in]
    out_specs = [pl.BlockSpec((ngroups, cpt, kc), lambda i: (0, i, 0))]
    out_shape = [jax.ShapeDtypeStruct((ngroups, nseg * rps // T_CHUNK, kc), BF16)]
    if latent:
        cw = (n_in - s5w) // 2
        out_specs.append(pl.BlockSpec((nseg, rseg, cw), lambda i: (0, i, 0)))
        out_shape.append(jax.ShapeDtypeStruct((nseg, rps, cw), F32))
    return pl.pallas_call(
        functools.partial(_inproj_kernel, latent=latent, s5w=s5w),
        grid=(rps // rseg,),
        in_specs=in_specs,
        out_specs=out_specs,
        out_shape=out_shape,
        scratch_shapes=[pltpu.VMEM((tm, d), BF16)] + [pltpu.VMEM((tm, LANES), F32)] * (s5w // LANES),
        compiler_params=_cparams(("arbitrary",)),
        name="in_proj" if latent else "in_proj_ctx",
    )(*args)


def _cmul_add(ar, ai, sr, si, zr, zi):
    return ar * sr - ai * si + zr, ar * si + ai * sr + zi


def _shift_right(lo, hi, s):
    lane = lax.broadcasted_iota(jnp.int32, lo.shape, 1)
    zero = jnp.zeros_like(lo)
    if s == 0:
        return lo, hi
    if s >= LANES:
        r = s - LANES
        return zero, (lo if r == 0 else jnp.where(lane >= r, pltpu.roll(lo, r, 1), zero))
    rl, rh = pltpu.roll(lo, s, 1), pltpu.roll(hi, s, 1)
    return jnp.where(lane >= s, rl, zero), jnp.where(lane >= s, rh, rl)


def _shift_left(lo, hi, s):
    lane = lax.broadcasted_iota(jnp.int32, lo.shape, 1)
    zero = jnp.zeros_like(lo)
    if s == 0:
        return lo, hi
    if s >= LANES:
        r = s - LANES
        return (hi if r == 0 else jnp.where(lane < LANES - r, pltpu.roll(hi, LANES - r, 1), zero)), zero
    rl, rh = pltpu.roll(lo, LANES - s, 1), pltpu.roll(hi, LANES - s, 1)
    keep = lane < LANES - s
    return jnp.where(keep, rl, rh), jnp.where(keep, rh, zero)


def _cmul(a, b):
    return a[0] * b[0] - a[1] * b[1], a[0] * b[1] + a[1] * b[0]


def _split_bf16(v):
    hi = v.astype(BF16)
    return hi, (v - hi.astype(F32)).astype(BF16)


_P_BBR, _P_BBI, _P_CR, _P_CI, _P_POW_RE, _P_POW_IM = 0, 16, 32, 48, 64, 68


def _s5ops_kernel(p_ref, lt_ref, d_ref, wsum_ref, cmat_ref, mfb_ref, *, gb):
    npole2 = p_ref.shape[1]
    npole = npole2 // 2
    shape = (npole2, LANES)
    lane = lax.broadcasted_iota(jnp.int32, shape, 1)
    fwd_rows = lax.broadcasted_iota(jnp.int32, shape, 0) < npole
    slot = lane // S5_H
    bits = [((slot >> b) & 1) == 1 for b in range(3)]
    one = jnp.ones(shape, F32)
    zero = jnp.zeros(shape, F32)
    sub16 = lax.broadcasted_iota(jnp.int32, (S5_H, LANES), 0)
    lane16 = lax.broadcasted_iota(jnp.int32, (S5_H, LANES), 1)

    def tile16(v):
        v = jnp.where(lax.broadcasted_iota(jnp.int32, v.shape, 1) < S5_H, v, 0.0)
        for sh in (S5_H, 2 * S5_H, 4 * S5_H):
            v = v + pltpu.roll(v, sh, 1)
        return v

    for q in range(gb):
        x = p_ref[q]
        col = lambda k: jnp.broadcast_to(x[:, k:k + 1], shape)
        lam = [(col(_P_POW_RE + b), col(_P_POW_IM + b)) for b in range(4)]
        field = lambda o: tile16(x if o == 0 else pltpu.roll(x, LANES - o, 1))
        bb = field(_P_BBR), field(_P_BBI)
        cc = field(_P_CR), field(_P_CI)

        def low_power(masks):
            acc = None
            for b in range(3):
                f = jnp.where(masks[b], lam[b][0], one), jnp.where(masks[b], lam[b][1], zero)
                acc = f if acc is None else _cmul(acc, f)
            return acc

        up = low_power(bits)
        down = low_power([jnp.logical_not(m) for m in bits])
        e_tau = [up, _cmul(up, lam[3])]
        e_rev = [_cmul(down, lam[3]), down]
        e_tau1 = [_cmul(e, lam[0]) for e in e_tau]
        e_rev1 = [_cmul(e, lam[0]) for e in e_rev]

        def pick(f, b):
            return jnp.where(fwd_rows, f[0], b[0]), jnp.where(fwd_rows, f[1], b[1])

        rk = []
        for hf in range(2):
            cols = slice(hf * LANES, (hf + 1) * LANES)
            sr, si = _cmul(pick(e_rev[hf], e_tau[hf]), bb)
            wsum_ref[q, 0:npole2, cols] = sr.astype(BF16)
            wsum_ref[q, npole2:, cols] = si.astype(BF16)
            rr, ri = _cmul(pick(e_tau1[hf], e_rev1[hf]), cc)
            cmat_ref[q, 0:npole2, cols] = rr.astype(BF16)
            cmat_ref[q, npole2:, cols] = (-ri).astype(BF16)
            rk.append(_cmul(pick(e_tau[hf], e_rev[hf]), cc))

        kts = []
        for dr in range(2):
            rows = slice(dr * npole, (dr + 1) * npole)
            rhs = jnp.concatenate([jnp.concatenate([rk[0][part][rows], rk[1][part][rows]], axis=1)
                                   for part in range(2)], axis=0)
            l_hi, l_lo = _split_bf16(lt_ref[q, dr * S5_H:(dr + 1) * S5_H, :])
            r_hi, r_lo = _split_bf16(rhs)
            kts.append(_dot(l_hi, r_hi) + _dot(l_lo, r_hi) + _dot(l_hi, r_lo))
        kf_lo, kf_hi = kts[0][:, 0:LANES], kts[0][:, LANES:]
        kb_lo, kb_hi = kts[1][:, 0:LANES], kts[1][:, LANES:]

        dsk = tile16(jnp.broadcast_to(d_ref[q, 0:1, :], (S5_H, LANES)))
        for ti in range(T_CHUNK):
            f_lo, f_hi = _shift_right(kf_lo, kf_hi, ti * S5_H)
            b_lo, b_hi = _shift_left(kb_lo, kb_hi, (T_CHUNK - 1 - ti) * S5_H)
            diag = jnp.where(lane16 == (ti % 8) * S5_H + sub16, dsk, 0.0)
            lo = f_lo + b_lo + (diag if ti < 8 else 0.0)
            hi = f_hi + b_hi + (diag if ti >= 8 else 0.0)
            mfb_ref[q, ti * S5_H:(ti + 1) * S5_H, :] = jnp.concatenate([lo, hi], axis=1).astype(BF16)


def _s5ops(pblk, lt, dvec, *, gb):
    ngroups, npole2, _ = pblk.shape
    kc = T_CHUNK * S5_H
    blk = lambda a: pl.BlockSpec((gb,) + a.shape[1:], lambda i: (i,) + (0,) * (a.ndim - 1))
    out = jax.ShapeDtypeStruct((ngroups, 2 * npole2, kc), BF16)
    return pl.pallas_call(
        functools.partial(_s5ops_kernel, gb=gb),
        grid=(ngroups // gb,),
        in_specs=[blk(pblk), blk(lt), blk(dvec)],
        out_specs=[blk(out), blk(out), blk(out)],
        out_shape=[out, out, out],
        compiler_params=_cparams(("arbitrary",)),
        name="s5_ops",
    )(pblk, lt, dvec)


def _s5core_kernel(u2_ref, u2c_ref, wsum_ref, mfb_ref, cmat_ref, a_ref, y2_ref, z_s, s_s, car_s, *, gb, cps, ncc):
    half = LANES // 2
    isf8 = lax.broadcasted_iota(jnp.int32, (SUBLANES, LANES), 1) < half
    isf1 = lax.broadcasted_iota(jnp.int32, (1, LANES), 1) < half

    for q in range(gb):
        z_s[q] = _dot_nt(u2_ref[q], wsum_ref[q])

    ctx_fin = []
    for q in range(gb):
        zc = _dot_nt(u2c_ref[q], wsum_ref[q])
        ar, ai = a_ref[q, 0:1, :], a_ref[q, 1:2, :]
        sr = jnp.zeros((1, LANES), F32)
        si = jnp.zeros((1, LANES), F32)
        for k in range(ncc):
            kb = ncc - 1 - k
            zr = jnp.where(isf1, zc[k:k + 1, 0:LANES], zc[kb:kb + 1, 0:LANES])
            zi = jnp.where(isf1, zc[k:k + 1, LANES:], zc[kb:kb + 1, LANES:])
            sr, si = _cmul_add(ar, ai, sr, si, zr, zi)
        ctx_fin.append((sr, si))

    def body1(k, carry):
        rf = pl.multiple_of(k * SUBLANES, SUBLANES)
        rb = pl.multiple_of((cps - 1 - k) * SUBLANES, SUBLANES)
        new = []
        for q in range(gb):
            sr, si = carry[q]
            s_s[q, pl.ds(rf, SUBLANES), 0:half] = sr[:, 0:half]
            s_s[q, pl.ds(rb, SUBLANES), half:LANES] = sr[:, half:]
            s_s[q, pl.ds(rf, SUBLANES), LANES:LANES + half] = si[:, 0:half]
            s_s[q, pl.ds(rb, SUBLANES), LANES + half:] = si[:, half:]
            zr = jnp.where(isf8, z_s[q, pl.ds(rf, SUBLANES), 0:LANES], z_s[q, pl.ds(rb, SUBLANES), 0:LANES])
            zi = jnp.where(isf8, z_s[q, pl.ds(rf, SUBLANES), LANES:], z_s[q, pl.ds(rb, SUBLANES), LANES:])
            new.append(_cmul_add(a_ref[q, 0:1, :], a_ref[q, 1:2, :], sr, si, zr, zi))
        return tuple(new)

    zero8 = jnp.zeros((SUBLANES, LANES), F32)
    fin = lax.fori_loop(0, cps, body1, tuple((zero8, zero8) for _ in range(gb)))

    for q in range(gb):
        fr, fi = fin[q]
        asr, asi = a_ref[q, 2:3, :], a_ref[q, 3:4, :]
        cr, ci = ctx_fin[q]
        car_s[q, 0, 0:1, 0:half] = cr[:, 0:half]
        car_s[q, 0, SEG - 1:SEG, half:] = cr[:, half:]
        car_s[q, 1, 0:1, 0:half] = ci[:, 0:half]
        car_s[q, 1, SEG - 1:SEG, half:] = ci[:, half:]
        for m in range(SEG - 1):
            mb = SEG - 1 - m
            zr = jnp.where(isf1, fr[m:m + 1, :], fr[mb:mb + 1, :])
            zi = jnp.where(isf1, fi[m:m + 1, :], fi[mb:mb + 1, :])
            cr, ci = _cmul_add(asr, asi, cr, ci, zr, zi)
            car_s[q, 0, m + 1:m + 2, 0:half] = cr[:, 0:half]
            car_s[q, 0, mb - 1:mb, half:] = cr[:, half:]
            car_s[q, 1, m + 1:m + 2, 0:half] = ci[:, 0:half]
            car_s[q, 1, mb - 1:mb, half:] = ci[:, half:]

    def body2(k, pw):
        rf = pl.multiple_of(k * SUBLANES, SUBLANES)
        rb = pl.multiple_of((cps - 1 - k) * SUBLANES, SUBLANES)
        new = []
        for q in range(gb):
            pr, pi = pw[q]
            cr, ci = car_s[q, 0], car_s[q, 1]
            tr = pr * cr - pi * ci
            ti = pr * ci + pi * cr
            s_s[q, pl.ds(rf, SUBLANES), 0:half] += tr[:, 0:half]
            s_s[q, pl.ds(rb, SUBLANES), half:LANES] += tr[:, half:]
            s_s[q, pl.ds(rf, SUBLANES), LANES:LANES + half] += ti[:, 0:half]
            s_s[q, pl.ds(rb, SUBLANES), LANES + half:] += ti[:, half:]
            ar, ai = a_ref[q, 0:1, :], a_ref[q, 1:2, :]
            new.append((pr * ar - pi * ai, pr * ai + pi * ar))
        return tuple(new)

    one8 = jnp.ones((SUBLANES, LANES), F32)
    lax.fori_loop(0, cps, body2, tuple((one8, zero8) for _ in range(gb)))

    for q in range(gb):
        y2_ref[q] = _dot(u2_ref[q], mfb_ref[q]) + _dot(s_s[q].astype(BF16), cmat_ref[q])


def _s5core(u2, u2c, wsum_t, mfb, cmat, avec, *, gb, cps):
    ngroups, nrows, kc = u2.shape
    ncc = u2c.shape[1]
    ns = wsum_t.shape[1]
    blk = lambda a: pl.BlockSpec((gb,) + a.shape[1:], lambda i: (i,) + (0,) * (a.ndim - 1))
    return pl.pallas_call(
        functools.partial(_s5core_kernel, gb=gb, cps=cps, ncc=ncc),
        grid=(ngroups // gb,),
        in_specs=[blk(u2), blk(u2c), blk(wsum_t), blk(mfb), blk(cmat), blk(avec)],
        out_specs=pl.BlockSpec((gb, nrows, kc), lambda i: (i, 0, 0)),
        out_shape=jax.ShapeDtypeStruct((ngroups, nrows, kc), F32),
        scratch_shapes=[pltpu.VMEM((gb, nrows, ns), F32), pltpu.VMEM((gb, nrows, ns), F32),
                        pltpu.VMEM((gb, 2, SUBLANES, LANES), F32)],
        compiler_params=_cparams(("arbitrary",)),
        name="s5_core",
    )(u2, u2c, wsum_t, mfb, cmat, avec)


CONV_HALO = 16


def _outproj_kernel(x_ref, rt_ref, ct_ref, y2_ref, vb_ref, vbp_ref, vbn_ref, wglu_ref, cw_ref, cb_ref,
                    lng_ref, lnb_ref, wout_ref, g1_ref, o_ref, ext_s, cat_s, *ys_s):
    i = pl.program_id(0)
    last = pl.num_programs(0) - 1
    nseg, rseg, _ = x_ref.shape
    tm = nseg * rseg
    s5w = len(ys_s) * LANES
    cwid = vb_ref.shape[2]
    _from_chunked(y2_ref, ys_s, rseg // T_CHUNK, nseg, rseg)

    wglu = wglu_ref[...].astype(BF16)
    rb = 128
    for r in range(tm // rb):
        rows = slice(r * rb, (r + 1) * rb)
        ya = jax.nn.gelu(jnp.concatenate([ys_ref[rows, :] for ys_ref in ys_s], axis=1))
        ya = ya * jax.nn.sigmoid(_dot(ya.astype(BF16), wglu))
        cat_s[rows, 0:s5w] = ya.astype(BF16)

    zeros_h = jnp.zeros((CONV_HALO, cwid), F32)
    pad = CONV_K // 2
    ext_rows = rseg + 2 * CONV_HALO
    nacc = rseg + SUBLANES
    for seg in range(nseg):
        prev_first = vbp_ref[seg - 1] if seg > 0 else zeros_h
        next_last = vbn_ref[seg + 1] if seg < nseg - 1 else zeros_h
        ext_s[0:CONV_HALO, :] = jnp.where(i > 0, vbp_ref[seg], prev_first)
        ext_s[CONV_HALO:CONV_HALO + rseg, :] = vb_ref[seg]
        ext_s[CONV_HALO + rseg:ext_rows, :] = jnp.where(i < last, vbn_ref[seg], next_last)
        acc = jnp.broadcast_to(cb_ref[...], (rseg, cwid))
        for s in range(SUBLANES):
            part = None
            for qq in range(ext_rows // SUBLANES):
                k = qq * SUBLANES + s - (CONV_HALO - pad)
                if k < 0 or k >= CONV_K or qq * SUBLANES + nacc > ext_rows:
                    continue
                term = cw_ref[k:k + 1, :] * ext_s[qq * SUBLANES:qq * SUBLANES + nacc, :]
                part = term if part is None else part + term
            acc = acc + part[s:s + rseg, :]
        mu = jnp.mean(acc, axis=-1, keepdims=True)
        xc = acc - mu
        var = jnp.mean(xc * xc, axis=-1, keepdims=True)
        y = xc * lax.rsqrt(var + EPS) * lng_ref[...] + lnb_ref[...]
        y = y * jax.nn.sigmoid(y)
        cat_s[seg * rseg:(seg + 1) * rseg, s5w:] = y.astype(BF16)

    yx = _dot(cat_s[...], wout_ref[...].astype(BF16))
    g1 = g1_ref[...]
    for seg in range(nseg):
        h = _add_pos(x_ref[seg], rt_ref, ct_ref, seg)
        o_ref[seg] = h + g1 * yx[seg * rseg:(seg + 1) * rseg, :]


def _outproj(x3, tabs, y2, vb3, wglu, conv_w, conv_b, ln_g, ln_b, wout, g1):
    nseg, rps, d = x3.shape
    ngroups, _, kc = y2.shape
    s5w = ngroups * S5_H
    cw = vb3.shape[2]
    rseg = SLAB
    tm = nseg * rseg
    rt, ct = tabs
    hb = rseg // CONV_HALO
    nhb = rps // CONV_HALO
    vec = lambda a: pl.BlockSpec(a.shape, lambda i: (0,) * a.ndim)
    tile = lambda w: pl.BlockSpec((nseg, rseg, w), lambda i: (0, i, 0))
    return pl.pallas_call(
        _outproj_kernel,
        grid=(rps // rseg,),
        in_specs=[
            tile(d),
            pl.BlockSpec((None, nseg, rt.shape[2]), lambda i: (i, 0, 0)),
            _resident(ct),
            pl.BlockSpec((ngroups, tm // T_CHUNK, kc), lambda i: (0, i, 0)),
            tile(cw),
            pl.BlockSpec((nseg, CONV_HALO, cw), lambda i: (0, (i * hb + nhb - 1) % nhb, 0)),
            pl.BlockSpec((nseg, CONV_HALO, cw), lambda i: (0, ((i + 1) * hb) % nhb, 0)),
            _resident(wglu), vec(conv_w), vec(conv_b), vec(ln_g), vec(ln_b), _resident(wout), vec(g1),
        ],
        out_specs=tile(d),
        out_shape=jax.ShapeDtypeStruct((nseg, rps, d), F32),
        scratch_shapes=[pltpu.VMEM((rseg + 2 * CONV_HALO, cw), F32), pltpu.VMEM((tm, d), BF16)]
        + [pltpu.VMEM((tm, LANES), F32)] * (s5w // LANES),
        compiler_params=_cparams(("arbitrary",)),
        name="out_proj",
    )(x3, rt, ct, y2, vb3, vb3, vb3, wglu, conv_w, conv_b, ln_g, ln_b, wout, g1)


def _mlp_kernel(h_ref, g_ref, sh_ref, sc_ref, g2_ref, w1_ref, w2_ref, fg_ref, o_ref, *, ff_chunk, final):
    h = h_ref[...]
    gm = g_ref[...] * (1.0 + sc_ref[...])
    nb = _rms_mod(h, gm, sh_ref[...]).astype(BF16)
    acc = None
    for c in range(w1_ref.shape[1] // ff_chunk):
        cols = slice(c * ff_chunk, (c + 1) * ff_chunk)
        a = jnp.maximum(_dot(nb, w1_ref[:, cols].astype(BF16)), 0.0)
        p = _dot((a * a).astype(BF16), w2_ref[cols, :].astype(BF16))
        acc = p if acc is None else acc + p
    out = h + g2_ref[...] * acc
    if final:
        ms = jnp.mean(out * out, axis=-1, keepdims=True)
        out = out * lax.rsqrt(ms + EPS) * fg_ref[...]
    o_ref[...] = out


def _mlp(h, g, sh, sc, g2, w1, w2, fg, *, layer, tm, final):
    rows, d = h.shape
    vec = pl.BlockSpec((1, d), lambda i: (0, 0))
    slab = lambda w: pl.BlockSpec((None,) + w.shape[1:], lambda i: (layer, 0, 0), pipeline_mode=pl.Buffered(1))
    return pl.pallas_call(
        functools.partial(_mlp_kernel, ff_chunk=1024, final=final),
        grid=(rows // tm,),
        in_specs=[pl.BlockSpec((tm, d), lambda i: (i, 0)), vec, vec, vec, vec, slab(w1), slab(w2), vec],
        out_specs=pl.BlockSpec((tm, d), lambda i: (i, 0)),
        out_shape=jax.ShapeDtypeStruct((rows, d), F32),
        compiler_params=_cparams(("arbitrary",)),
        name="mlp_final" if final else "mlp",
    )(h, g, sh, sc, g2, w1, w2, fg)


POOL_HALO = 8


def _pool_kernel(h_ref, hp_ref, hn_ref, g_ref, sh_ref, sc_ref, g1_ref, pw_ref, pb_ref, ps_ref, o_ref, ext_s,
                 *, tm, seq_len):
    i = pl.program_id(0)
    last = pl.num_programs(0) - 1
    gm = g_ref[...] * (1.0 + sc_ref[...])
    sh = sh_ref[...]
    ext_s[0:POOL_HALO, :] = jnp.where(i > 0, _rms_mod(hp_ref[...], gm, sh), 0.0)
    ext_s[POOL_HALO:POOL_HALO + tm, :] = _rms_mod(h_ref[...], gm, sh)
    ext_s[POOL_HALO + tm:, :] = jnp.where(i < last, _rms_mod(hn_ref[...], gm, sh), 0.0)
    t = i * tm + lax.broadcasted_iota(jnp.int32, (tm, 1), 0)
    pc = pw_ref.shape[1]
    for gi, win in enumerate(POOL_WINDOWS):
        cols = slice(gi * pc, (gi + 1) * pc)
        left = win // 2
        right = win - 1 - left
        s = None
        for off in range(-left, right + 1):
            v = ext_s[POOL_HALO + off:POOL_HALO + off + tm, cols]
            s = v if s is None else s + v
        lo = jnp.maximum(t - left, 0)
        hi = jnp.minimum(t + right, seq_len - 1)
        cnt = (hi - lo + 1).astype(F32)
        pg = s / cnt - ext_s[POOL_HALO:POOL_HALO + tm, cols]
        y = (_dot(pg.astype(BF16), pw_ref[gi].astype(BF16)) + pb_ref[:, cols]) * ps_ref[:, cols]
        o_ref[:, cols] = h_ref[:, cols] + g1_ref[:, cols] * y


def _pool(h, g, sh, sc, g1, pool_w, pool_b, pool_scale, *, tm):
    rows, d = h.shape
    hb = tm // POOL_HALO
    nhb = rows // POOL_HALO
    vec = pl.BlockSpec((1, d), lambda i: (0, 0))
    return pl.pallas_call(
        functools.partial(_pool_kernel, tm=tm, seq_len=rows),
        grid=(rows // tm,),
        in_specs=[
            pl.BlockSpec((tm, d), lambda i: (i, 0)),
            pl.BlockSpec((POOL_HALO, d), lambda i: (jnp.maximum(i * hb - 1, 0), 0)),
            pl.BlockSpec((POOL_HALO, d), lambda i: (jnp.minimum((i + 1) * hb, nhb - 1), 0)),
            vec, vec, vec, vec, _resident(pool_w), vec, vec,
        ],
        out_specs=pl.BlockSpec((tm, d), lambda i: (i, 0)),
        out_shape=jax.ShapeDtypeStruct((rows, d), F32),
        scratch_shapes=[pltpu.VMEM((tm + 2 * POOL_HALO, d), F32)],
        compiler_params=_cparams(("arbitrary",)),
        name="pool_mixer",
    )(h, h, h, g, sh, sc, g1, pool_w, pool_b, pool_scale)


def _pos_tables(rows, d, nseg):
    quarter = d // 4
    omega = 1.0 / (10000.0 ** (jnp.arange(quarter, dtype=F32) / quarter))

    def emb(n):
        ang = jnp.arange(n, dtype=F32)[:, None] * omega[None, :]
        return jnp.concatenate([jnp.sin(ang), jnp.cos(ang)], axis=-1)

    rt = jnp.swapaxes(emb(rows).reshape(nseg, rows // nseg, 2 * quarter), 0, 1)
    return rt, emb(GRID_W)


def _s5_params(lam_re, lam_im, log_step, b_re, b_im, c_re, c_im, d_skip, cps):
    ngroups, npole = lam_re.shape[1:]
    t = T_CHUNK
    lr = jnp.minimum(lam_re.astype(F32), LAMBDA_RE_MAX)
    li = lam_im.astype(F32)
    dt = jnp.exp(log_step.astype(F32))[..., None]
    ldr, ldi = lr * dt, li * dt

    def cpow(n):
        mag = jnp.exp(ldr * n)
        return mag * jnp.cos(ldi * n), mag * jnp.sin(ldi * n)

    lbr, lbi = cpow(1.0)
    den = lr * lr + li * li
    qr = ((lbr - 1.0) * lr + lbi * li) / den
    qi = (lbi * lr - (lbr - 1.0) * li) / den
    br, bi = b_re.astype(F32), b_im.astype(F32)
    bbr = qr[..., None] * br - qi[..., None] * bi
    bbi = qr[..., None] * bi + qi[..., None] * br
    cr = jnp.swapaxes(c_re.astype(F32), 2, 3)
    ci = jnp.swapaxes(c_im.astype(F32), 2, 3)

    pows = [cpow(float(2 ** b)) for b in range(4)]
    pow_re = jnp.stack([p[0] for p in pows], axis=-1)
    pow_im = jnp.stack([p[1] for p in pows], axis=-1)
    fill = jnp.zeros(bbr.shape[:3] + (LANES - 4 * S5_H - 8,), F32)
    pblk = jnp.concatenate([bbr, bbi, cr, ci, pow_re, pow_im, fill], axis=-1)
    pblk = jnp.moveaxis(pblk, 0, 1).reshape(ngroups, 2 * npole, LANES)

    lt = jnp.concatenate([jnp.swapaxes(bbr, 2, 3), -jnp.swapaxes(bbi, 2, 3)], axis=-1)
    lt = jnp.moveaxis(lt, 0, 1).reshape(ngroups, 2 * S5_H, 2 * npole)

    dvec = jnp.zeros((ngroups, SUBLANES, LANES), F32).at[:, 0, :S5_H].set(d_skip.astype(F32).reshape(ngroups, S5_H))

    atr, ati = cpow(float(t))
    asr, asi = cpow(float(t * cps))
    cat2 = lambda v: jnp.concatenate([v[0], v[1]], axis=-1)
    avec = jnp.stack([cat2(atr), cat2(ati), cat2(asr), cat2(asi)], axis=1)
    avec = jnp.concatenate([avec, jnp.zeros((ngroups, SUBLANES - 4, 2 * npole), F32)], axis=1)
    return pblk, lt, dvec, avec


def kernel(x, c, ctx, c_ctx, w_ada, b_ada, norm_mix_g, norm_mlp_g, w_in, w_out, s5_lam_re, s5_lam_im, s5_log_step, s5_b_re, s5_b_im, s5_c_re, s5_c_im, s5_d, s5_w_glu, conv_w, conv_b, conv_ln_g, conv_ln_b, pool_w, pool_b, pool_scale, mlp_w1, mlp_w2, final_g):
    bsz, seq, d = x.shape
    assert bsz == 1 and c.shape[0] == 1 and ctx.shape[0] == 1
    assert w_ada.shape[0] == 2, "the kernels implement the depth-2 block (one even layer, one odd layer)"
    s5w = s5_d.shape[1]
    rps = seq // SEG
    cps = rps // T_CHUNK
    tm = 512

    x3 = x.reshape(SEG, rps, d)
    tabs = _pos_tables(seq // GRID_W, d, SEG)

    cc = jnp.zeros((16, d), F32).at[0].set(c[0].astype(F32)).at[1].set(c_ctx.astype(F32))
    mods = _ada(cc, w_ada, b_ada)

    def mod(layer, row, which):
        return lax.slice(mods[layer], (row, which * d), (row + 1, (which + 1) * d))

    row_vec = lambda v: v.reshape(1, -1).astype(F32)

    pblk, lt, dvec, avec = _s5_params(s5_lam_re[0], s5_lam_im[0], s5_log_step[0], s5_b_re[0], s5_b_im[0],
                                      s5_c_re[0], s5_c_im[0], s5_d[0], cps)
    wsum_t, cmat, mfb = _s5ops(pblk, lt, dvec, gb=4)
    g_mix0 = row_vec(norm_mix_g[0])
    u2, vb3 = _inproj(x3, tabs, g_mix0, mod(0, 0, 0), mod(0, 0, 1), w_in[0], s5w=s5w)
    u2c, = _inproj(ctx, None, g_mix0, mod(0, 1, 0), mod(0, 1, 1), w_in[0], s5w=s5w)
    y2 = _s5core(u2, u2c, wsum_t, mfb, cmat, avec, gb=4, cps=cps)
    h3 = _outproj(x3, tabs, y2, vb3, s5_w_glu[0], conv_w[0].astype(F32), row_vec(conv_b[0]),
                  row_vec(conv_ln_g[0]), row_vec(conv_ln_b[0]), w_out[0], mod(0, 0, 2))
    h = h3.reshape(seq, d)
    fg = row_vec(final_g)
    h = _mlp(h, row_vec(norm_mlp_g[0]), mod(0, 0, 3), mod(0, 0, 4), mod(0, 0, 5),
             mlp_w1, mlp_w2, fg, layer=0, tm=tm, final=False)

    h = _pool(h, row_vec(norm_mix_g[1]), mod(1, 0, 0), mod(1, 0, 1), mod(1, 0, 2),
              pool_w[0], row_vec(pool_b[0]), row_vec(pool_scale[0]), tm=tm)
    h = _mlp(h, row_vec(norm_mlp_g[1]), mod(1, 0, 3), mod(1, 0, 4), mod(1, 0, 5),
             mlp_w1, mlp_w2, fg, layer=1, tm=tm, final=True)
    return h[None]
```

```python
import functools

import jax
import jax.numpy as jnp
from jax import lax
from jax.experimental import pallas as pl
from jax.experimental.pallas import tpu as pltpu

GRID_W = 64
S5_H = 16
CONV_K = 31
POOL_WINDOWS = (2, 4, 8, 16)
EPS = 1e-6
LAMBDA_RE_MAX = -1e-4

LANES = 128
SUBLANES = 8
T_CHUNK = 16
SEG = SUBLANES
SLAB = GRID_W
GROUPS_PER_LANEBLOCK = LANES // S5_H
VMEM_LIMIT = 60 * 1024 * 1024

F32 = jnp.float32
BF16 = jnp.bfloat16


def _dot(a, b):
    return jnp.dot(a, b, preferred_element_type=F32)


def _dot_nt(a, b):
    return lax.dot_general(a, b, (((1,), (1,)), ((), ())), preferred_element_type=F32)


def _cparams(sem):
    return pltpu.CompilerParams(dimension_semantics=sem, vmem_limit_bytes=VMEM_LIMIT)


def _resident(a):
    return pl.BlockSpec(a.shape, lambda *_: (0,) * a.ndim, pipeline_mode=pl.Buffered(1))


def _ada_kernel(cc_ref, w_ref, b_ref, o_ref):
    cc = cc_ref[...]
    s = cc * jax.nn.sigmoid(cc)
    s_hi = s.astype(BF16)
    s_lo = (s - s_hi.astype(F32)).astype(BF16)
    w = w_ref[...]
    w_hi = w.astype(BF16)
    w_lo = (w - w_hi.astype(F32)).astype(BF16)
    o_ref[...] = _dot(s_hi, w_hi) + _dot(s_lo, w_hi) + _dot(s_hi, w_lo) + b_ref[...]


def _ada(cc, w_ada, b_ada):
    depth, d, n = w_ada.shape
    tn = 1536
    rows = cc.shape[0]
    return pl.pallas_call(
        _ada_kernel,
        grid=(depth, n // tn),
        in_specs=[
            pl.BlockSpec((rows, d), lambda l, j: (0, 0)),
            pl.BlockSpec((None, d, tn), lambda l, j: (l, 0, j)),
            pl.BlockSpec((None, 1, tn), lambda l, j: (l, 0, j)),
        ],
        out_specs=pl.BlockSpec((None, rows, tn), lambda l, j: (l, 0, j)),
        out_shape=jax.ShapeDtypeStruct((depth, rows, n), F32),
        compiler_params=_cparams(("arbitrary", "arbitrary")),
        name="ada_mod",
    )(cc, w_ada, b_ada.reshape(depth, 1, n))


def _slot_ids(nrows):
    return lax.broadcasted_iota(jnp.int32, (nrows, LANES), 1) // S5_H


def _to_chunked(zs_refs, nblk, nchunk, stride):
    slot = _slot_ids(nchunk)
    halves = T_CHUNK // GROUPS_PER_LANEBLOCK
    outs = {}
    for j, zs_ref in enumerate(zs_refs):
        for b in range(nblk):
            for m in range(halves):
                pieces = [zs_ref[pl.ds(b * T_CHUNK + m * 8 + tl, nchunk, stride=stride), :] for tl in range(8)]
                for i in range(8):
                    acc = None
                    for tl in range(8):
                        sh = ((tl - i) % 8) * S5_H
                        r = pieces[tl] if sh == 0 else pltpu.roll(pieces[tl], sh, 1)
                        acc = r if acc is None else jnp.where(slot == tl, r, acc)
                    outs[(8 * j + i, b, m)] = acc
    res = []
    for g in range(len(zs_refs) * GROUPS_PER_LANEBLOCK):
        rows = [jnp.concatenate([outs[(g, b, m)] for m in range(halves)], axis=1) for b in range(nblk)]
        res.append(rows[0] if nblk == 1 else jnp.concatenate(rows, axis=0))
    return res


def _from_chunked(y2_ref, ys_refs, nblk, nchunk, stride):
    slot = _slot_ids(nchunk)
    halves = T_CHUNK // GROUPS_PER_LANEBLOCK
    for j, ys_ref in enumerate(ys_refs):
        for b in range(nblk):
            for m in range(halves):
                srcs = [y2_ref[8 * j + i, b * nchunk:(b + 1) * nchunk, m * LANES:(m + 1) * LANES] for i in range(8)]
                for tl in range(8):
                    acc = None
                    for i in range(8):
                        sh = ((i - tl) % 8) * S5_H
                        r = srcs[i] if sh == 0 else pltpu.roll(srcs[i], sh, 1)
                        acc = r if acc is None else jnp.where(slot == i, r, acc)
                    ys_ref[pl.ds(b * T_CHUNK + m * 8 + tl, nchunk, stride=stride), :] = acc


def _add_pos(xs, rt_ref, ct_ref, seg):
    half = ct_ref.shape[1]
    pos = jnp.concatenate([jnp.broadcast_to(rt_ref[seg:seg + 1, :], (GRID_W, half)), ct_ref[...]], axis=1)
    return xs + pos


def _rms_mod(h, gm, sh):
    ms = jnp.mean(h * h, axis=-1, keepdims=True)
    return h * lax.rsqrt(ms + EPS) * gm + sh


def _inproj_kernel(*refs, latent, s5w):
    if latent:
        x_ref, rt_ref, ct_ref, g_ref, sh_ref, sc_ref, w_ref, u2_ref, vb_ref, n_s, *z_s = refs
    else:
        x_ref, g_ref, sh_ref, sc_ref, w_ref, u2_ref, n_s, *z_s = refs
    nseg, rseg, _ = x_ref.shape
    gm = g_ref[...] * (1.0 + sc_ref[...])
    sh = sh_ref[...]
    for seg in range(nseg):
        for r in range(rseg // GRID_W):
            xs = x_ref[seg, r * GRID_W:(r + 1) * GRID_W, :]
            if latent:
                xs = _add_pos(xs, rt_ref, ct_ref, seg)
            row0 = seg * rseg + r * GRID_W
            n_s[row0:row0 + GRID_W, :] = _rms_mod(xs, gm, sh).astype(BF16)
    z = _dot(n_s[...], w_ref[...].astype(BF16))
    for j, zs_ref in enumerate(z_s):
        zs_ref[...] = z[:, j * LANES:(j + 1) * LANES]
    if latent:
        cw = (z.shape[1] - s5w) // 2
        vb = z[:, s5w:s5w + cw] * jax.nn.sigmoid(z[:, s5w + cw:])
        for seg in range(nseg):
            vb_ref[seg] = vb[seg * rseg:(seg + 1) * rseg, :]
        chunks = _to_chunked(z_s, rseg // T_CHUNK, nseg, rseg)
    else:
        chunks = _to_chunked(z_s, 1, rseg // T_CHUNK, T_CHUNK)
    for g, ch in enumerate(chunks):
        u2_ref[g] = ch.astype(BF16)


def _inproj(x3, tabs, g, sh, sc, w_in, *, s5w):
    nseg, rps, d = x3.shape
    n_in = w_in.shape[1]
    ngroups = s5w // S5_H
    kc = T_CHUNK * S5_H
    latent = tabs is not None
    rseg = SLAB if latent else rps
    tm = nseg * rseg
    cpt = tm // T_CHUNK
    vec = pl.BlockSpec((1, d), lambda i: (0, 0))
    in_specs = [pl.BlockSpec((nseg, rseg, d), lambda i: (0, i, 0))]
    args = [x3]
    if latent:
        rt, ct = tabs
        in_specs += [pl.BlockSpec((None, nseg, rt.shape[2]), lambda i: (i, 0, 0)), _resident(ct)]
        args += [rt, ct]
    in_specs += [vec, vec, vec, _resident(w_in)]
    args += [g, sh, sc, w_in]
    out_specs = [pl.BlockSpec((ngroups, cpt, kc), lambda i: (0, i, 0))]
    out_shape = [jax.ShapeDtypeStruct((ngroups, nseg * rps // T_CHUNK, kc), BF16)]
    if latent:
        cw = (n_in - s5w) // 2
        out_specs.append(pl.BlockSpec((nseg, rseg, cw), lambda i: (0, i, 0)))
        out_shape.append(jax.ShapeDtypeStruct((nseg, rps, cw), F32))
    return pl.pallas_call(
        functools.partial(_inproj_kernel, latent=latent, s5w=s5w),
        grid=(rps // rseg,),
        in_specs=in_specs,
        out_specs=out_specs,
        out_shape=out_shape,
        scratch_shapes=[pltpu.VMEM((tm, d), BF16)] + [pltpu.VMEM((tm, LANES), F32)] * (s5w // LANES),
        compiler_params=_cparams(("arbitrary",)),
        name="in_proj" if latent else "in_proj_ctx",
    )(*args)


def _cmul_add(ar, ai, sr, si, zr, zi):
    return ar * sr - ai * si + zr, ar * si + ai * sr + zi


def _shift_right(lo, hi, s):
    lane = lax.broadcasted_iota(jnp.int32, lo.shape, 1)
    zero = jnp.zeros_like(lo)
    if s == 0:
        return lo, hi
    if s >= LANES:
        r = s - LANES
        return zero, (lo if r == 0 else jnp.where(lane >= r, pltpu.roll(lo, r, 1), zero))
    rl, rh = pltpu.roll(lo, s, 1), pltpu.roll(hi, s, 1)
    return jnp.where(lane >= s, rl, zero), jnp.where(lane >= s, rh, rl)


def _shift_left(lo, hi, s):
    lane = lax.broadcasted_iota(jnp.int32, lo.shape, 1)
    zero = jnp.zeros_like(lo)
    if s == 0:
        return lo, hi
    if s >= LANES:
        r = s - LANES
        return (hi if r == 0 else jnp.where(lane < LANES - r, pltpu.roll(hi, LANES - r, 1), zero)), zero
    rl, rh = pltpu.roll(lo, LANES - s, 1), pltpu.roll(hi, LANES - s, 1)
    keep = lane < LANES - s
    return jnp.where(keep, rl, rh), jnp.where(keep, rh, zero)


def _cmul(a, b):
    return a[0] * b[0] - a[1] * b[1], a[0] * b[1] + a[1] * b[0]


def _split_bf16(v):
    hi = v.astype(BF16)
    return hi, (v - hi.astype(F32)).astype(BF16)


_P_BBR, _P_BBI, _P_CR, _P_CI, _P_POW_RE, _P_POW_IM = 0, 16, 32, 48, 64, 68


def _s5ops_kernel(p_ref, lt_ref, d_ref, wsum_ref, cmat_ref, mfb_ref, *, gb):
    npole2 = p_ref.shape[1]
    npole = npole2 // 2
    shape = (npole2, LANES)
    lane = lax.broadcasted_iota(jnp.int32, shape, 1)
    fwd_rows = lax.broadcasted_iota(jnp.int32, shape, 0) < npole
    slot = lane // S5_H
    bits = [((slot >> b) & 1) == 1 for b in range(3)]
    one = jnp.ones(shape, F32)
    zero = jnp.zeros(shape, F32)
    sub16 = lax.broadcasted_iota(jnp.int32, (S5_H, LANES), 0)
    lane16 = lax.broadcasted_iota(jnp.int32, (S5_H, LANES), 1)

    def tile16(v):
        v = jnp.where(lax.broadcasted_iota(jnp.int32, v.shape, 1) < S5_H, v, 0.0)
        for sh in (S5_H, 2 * S5_H, 4 * S5_H):
            v = v + pltpu.roll(v, sh, 1)
        return v

    for q in range(gb):
        x = p_ref[q]
        col = lambda k: jnp.broadcast_to(x[:, k:k + 1], shape)
        lam = [(col(_P_POW_RE + b), col(_P_POW_IM + b)) for b in range(4)]
        field = lambda o: tile16(x if o == 0 else pltpu.roll(x, LANES - o, 1))
        bb = field(_P_BBR), field(_P_BBI)
        cc = field(_P_CR), field(_P_CI)

        def low_power(masks):
            acc = None
            for b in range(3):
                f = jnp.where(masks[b], lam[b][0], one), jnp.where(masks[b], lam[b][1], zero)
                acc = f if acc is None else _cmul(acc, f)
            return acc

        up = low_power(bits)
        down = low_power([jnp.logical_not(m) for m in bits])
        e_tau = [up, _cmul(up, lam[3])]
        e_rev = [_cmul(down, lam[3]), down]
        e_tau1 = [_cmul(e, lam[0]) for e in e_tau]
        e_rev1 = [_cmul(e, lam[0]) for e in e_rev]

        def pick(f, b):
            return jnp.where(fwd_rows, f[0], b[0]), jnp.where(fwd_rows, f[1], b[1])

        rk = []
        for hf in range(2):
            cols = slice(hf * LANES, (hf + 1) * LANES)
            sr, si = _cmul(pick(e_rev[hf], e_tau[hf]), bb)
            wsum_ref[q, 0:npole2, cols] = sr.astype(BF16)
            wsum_ref[q, npole2:, cols] = si.astype(BF16)
            rr, ri = _cmul(pick(e_tau1[hf], e_rev1[hf]), cc)
            cmat_ref[q, 0:npole2, cols] = rr.astype(BF16)
            cmat_ref[q, npole2:, cols] = (-ri).astype(BF16)
            rk.append(_cmul(pick(e_tau[hf], e_rev[hf]), cc))

        kts = []
        for dr in range(2):
            rows = slice(dr * npole, (dr + 1) * npole)
            rhs = jnp.concatenate([jnp.concatenate([rk[0][part][rows], rk[1][part][rows]], axis=1)
                                   for part in range(2)], axis=0)
            l_hi, l_lo = _split_bf16(lt_ref[q, dr * S5_H:(dr + 1) * S5_H, :])
            r_hi, r_lo = _split_bf16(rhs)
            kts.append(_dot(l_hi, r_hi) + _dot(l_lo, r_hi) + _dot(l_hi, r_lo))
        kf_lo, kf_hi = kts[0][:, 0:LANES], kts[0][:, LANES:]
        kb_lo, kb_hi = kts[1][:, 0:LANES], kts[1][:, LANES:]

        dsk = tile16(jnp.broadcast_to(d_ref[q, 0:1, :], (S5_H, LANES)))
        for ti in range(T_CHUNK):
            f_lo, f_hi = _shift_right(kf_lo, kf_hi, ti * S5_H)
            b_lo, b_hi = _shift_left(kb_lo, kb_hi, (T_CHUNK - 1 - ti) * S5_H)
            diag = jnp.where(lane16 == (ti % 8) * S5_H + sub16, dsk, 0.0)
            lo = f_lo + b_lo + (diag if ti < 8 else 0.0)
            hi = f_hi + b_hi + (diag if ti >= 8 else 0.0)
            mfb_ref[q, ti * S5_H:(ti + 1) * S5_H, :] = jnp.concatenate([lo, hi], axis=1).astype(BF16)


def _s5ops(pblk, lt, dvec, *, gb):
    ngroups, npole2, _ = pblk.shape
    kc = T_CHUNK * S5_H
    blk = lambda a: pl.BlockSpec((gb,) + a.shape[1:], lambda i: (i,) + (0,) * (a.ndim - 1))
    out = jax.ShapeDtypeStruct((ngroups, 2 * npole2, kc), BF16)
    return pl.pallas_call(
        functools.partial(_s5ops_kernel, gb=gb),
        grid=(ngroups // gb,),
        in_specs=[blk(pblk), blk(lt), blk(dvec)],
        out_specs=[blk(out), blk(out), blk(out)],
        out_shape=[out, out, out],
        compiler_params=_cparams(("arbitrary",)),
        name="s5_ops",
    )(pblk, lt, dvec)


def _s5core_kernel(u2_ref, u2c_ref, wsum_ref, mfb_ref, cmat_ref, a_ref, y2_ref, z_s, s_s, car_s, *, gb, cps, ncc):
    half = LANES // 2
    isf8 = lax.broadcasted_iota(jnp.int32, (SUBLANES, LANES), 1) < half
    isf1 = lax.broadcasted_iota(jnp.int32, (1, LANES), 1) < half

    for q in range(gb):
        z_s[q] = _dot_nt(u2_ref[q], wsum_ref[q])

    ctx_fin = []
    for q in range(gb):
        zc = _dot_nt(u2c_ref[q], wsum_ref[q])
        ar, ai = a_ref[q, 0:1, :], a_ref[q, 1:2, :]
        sr = jnp.zeros((1, LANES), F32)
        si = jnp.zeros((1, LANES), F32)
        for k in range(ncc):
            kb = ncc - 1 - k
            zr = jnp.where(isf1, zc[k:k + 1, 0:LANES], zc[kb:kb + 1, 0:LANES])
            zi = jnp.where(isf1, zc[k:k + 1, LANES:], zc[kb:kb + 1, LANES:])
            sr, si = _cmul_add(ar, ai, sr, si, zr, zi)
        ctx_fin.append((sr, si))

    def body1(k, carry):
        rf = pl.multiple_of(k * SUBLANES, SUBLANES)
        rb = pl.multiple_of((cps - 1 - k) * SUBLANES, SUBLANES)
        new = []
        for q in range(gb):
            sr, si = carry[q]
            s_s[q, pl.ds(rf, SUBLANES), 0:half] = sr[:, 0:half]
            s_s[q, pl.ds(rb, SUBLANES), half:LANES] = sr[:, half:]
            s_s[q, pl.ds(rf, SUBLANES), LANES:LANES + half] = si[:, 0:half]
            s_s[q, pl.ds(rb, SUBLANES), LANES + half:] = si[:, half:]
            zr = jnp.where(isf8, z_s[q, pl.ds(rf, SUBLANES), 0:LANES], z_s[q, pl.ds(rb, SUBLANES), 0:LANES])
            zi = jnp.where(isf8, z_s[q, pl.ds(rf, SUBLANES), LANES:], z_s[q, pl.ds(rb, SUBLANES), LANES:])
            new.append(_cmul_add(a_ref[q, 0:1, :], a_ref[q, 1:2, :], sr, si, zr, zi))
        return tuple(new)

    zero8 = jnp.zeros((SUBLANES, LANES), F32)
    fin = lax.fori_loop(0, cps, body1, tuple((zero8, zero8) for _ in range(gb)))

    for q in range(gb):
        fr, fi = fin[q]
        asr, asi = a_ref[q, 2:3, :], a_ref[q, 3:4, :]
        cr, ci = ctx_fin[q]
        car_s[q, 0, 0:1, 0:half] = cr[:, 0:half]
        car_s[q, 0, SEG - 1:SEG, half:] = cr[:, half:]
        car_s[q, 1, 0:1, 0:half] = ci[:, 0:half]
        car_s[q, 1, SEG - 1:SEG, half:] = ci[:, half:]
        for m in range(SEG - 1):
            mb = SEG - 1 - m
            zr = jnp.where(isf1, fr[m:m + 1, :], fr[mb:mb + 1, :])
            zi = jnp.where(isf1, fi[m:m + 1, :], fi[mb:mb + 1, :])
            cr, ci = _cmul_add(asr, asi, cr, ci, zr, zi)
            car_s[q, 0, m + 1:m + 2, 0:half] = cr[:, 0:half]
            car_s[q, 0, mb - 1:mb, half:] = cr[:, half:]
            car_s[q, 1, m + 1:m + 2, 0:half] = ci[:, 0:half]
            car_s[q, 1, mb - 1:mb, half:] = ci[:, half:]

    def body2(k, pw):
        rf = pl.multiple_of(k * SUBLANES, SUBLANES)
        rb = pl.multiple_of((cps - 1 - k) * SUBLANES, SUBLANES)
        new = []
        for q in range(gb):
            pr, pi = pw[q]
            cr, ci = car_s[q, 0], car_s[q, 1]
            tr = pr * cr - pi * ci
            ti = pr * ci + pi * cr
            s_s[q, pl.ds(rf, SUBLANES), 0:half] += tr[:, 0:half]
            s_s[q, pl.ds(rb, SUBLANES), half:LANES] += tr[:, half:]
            s_s[q, pl.ds(rf, SUBLANES), LANES:LANES + half] += ti[:, 0:half]
            s_s[q, pl.ds(rb, SUBLANES), LANES + half:] += ti[:, half:]
            ar, ai = a_ref[q, 0:1, :], a_ref[q, 1:2, :]
            new.append((pr * ar - pi * ai, pr * ai + pi * ar))
        return tuple(new)

    one8 = jnp.ones((SUBLANES, LANES), F32)
    lax.fori_loop(0, cps, body2, tuple((one8, zero8) for _ in range(gb)))

    for q in range(gb):
        y2_ref[q] = _dot(u2_ref[q], mfb_ref[q]) + _dot(s_s[q].astype(BF16), cmat_ref[q])


def _s5core(u2, u2c, wsum_t, mfb, cmat, avec, *, gb, cps):
    ngroups, nrows, kc = u2.shape
    ncc = u2c.shape[1]
    ns = wsum_t.shape[1]
    blk = lambda a: pl.BlockSpec((gb,) + a.shape[1:], lambda i: (i,) + (0,) * (a.ndim - 1))
    return pl.pallas_call(
        functools.partial(_s5core_kernel, gb=gb, cps=cps, ncc=ncc),
        grid=(ngroups // gb,),
        in_specs=[blk(u2), blk(u2c), blk(wsum_t), blk(mfb), blk(cmat), blk(avec)],
        out_specs=pl.BlockSpec((gb, nrows, kc), lambda i: (i, 0, 0)),
        out_shape=jax.ShapeDtypeStruct((ngroups, nrows, kc), F32),
        scratch_shapes=[pltpu.VMEM((gb, nrows, ns), F32), pltpu.VMEM((gb, nrows, ns), F32),
                        pltpu.VMEM((gb, 2, SUBLANES, LANES), F32)],
        compiler_params=_cparams(("arbitrary",)),
        name="s5_core",
    )(u2, u2c, wsum_t, mfb, cmat, avec)


CONV_HALO = 16


def _mixer0_stage(i, last, x_ref, rt_ref, ct_ref, y2_ref, vb_ref, vbp_ref, vbn_ref, wglu_ref, cw_ref, cb_ref,
                  lng_ref, lnb_ref, wout_ref, g1_ref, h_dst, ext_s, cat_s, ys_s):
    nseg, rseg, _ = x_ref.shape
    tm = nseg * rseg
    s5w = len(ys_s) * LANES
    cwid = vb_ref.shape[2]
    _from_chunked(y2_ref, ys_s, rseg // T_CHUNK, nseg, rseg)

    wglu = wglu_ref[...].astype(BF16)
    rb = 128
    for r in range(tm // rb):
        rows = slice(r * rb, (r + 1) * rb)
        ya = jax.nn.gelu(jnp.concatenate([ys_ref[rows, :] for ys_ref in ys_s], axis=1))
        ya = ya * jax.nn.sigmoid(_dot(ya.astype(BF16), wglu))
        cat_s[rows, 0:s5w] = ya.astype(BF16)

    zeros_h = jnp.zeros((CONV_HALO, cwid), F32)
    pad = CONV_K // 2
    ext_rows = rseg + 2 * CONV_HALO
    nacc = rseg + SUBLANES
    for seg in range(nseg):
        prev_first = vbp_ref[seg - 1] if seg > 0 else zeros_h
        next_last = vbn_ref[seg + 1] if seg < nseg - 1 else zeros_h
        ext_s[0:CONV_HALO, :] = jnp.where(i > 0, vbp_ref[seg], prev_first)
        ext_s[CONV_HALO:CONV_HALO + rseg, :] = vb_ref[seg]
        ext_s[CONV_HALO + rseg:ext_rows, :] = jnp.where(i < last, vbn_ref[seg], next_last)
        acc = jnp.broadcast_to(cb_ref[...], (rseg, cwid))
        for s in range(SUBLANES):
            part = None
            for qq in range(ext_rows // SUBLANES):
                k = qq * SUBLANES + s - (CONV_HALO - pad)
                if k < 0 or k >= CONV_K or qq * SUBLANES + nacc > ext_rows:
                    continue
                term = cw_ref[k:k + 1, :] * ext_s[qq * SUBLANES:qq * SUBLANES + nacc, :]
                part = term if part is None else part + term
            acc = acc + part[s:s + rseg, :]
        mu = jnp.mean(acc, axis=-1, keepdims=True)
        xc = acc - mu
        var = jnp.mean(xc * xc, axis=-1, keepdims=True)
        y = xc * lax.rsqrt(var + EPS) * lng_ref[...] + lnb_ref[...]
        y = y * jax.nn.sigmoid(y)
        cat_s[seg * rseg:(seg + 1) * rseg, s5w:] = y.astype(BF16)

    yx = _dot(cat_s[...], wout_ref[...].astype(BF16))
    g1 = g1_ref[...]
    for seg in range(nseg):
        rows = slice(seg * rseg, (seg + 1) * rseg)
        h_dst[rows, :] = _add_pos(x_ref[seg], rt_ref, ct_ref, seg) + g1 * yx[rows, :]


FF_CHUNK = 1024


def _mlp_stage(h, g_ref, sh_ref, sc_ref, g2_ref, w1_ref, w2_ref, fg_ref, final):
    gm = g_ref[...] * (1.0 + sc_ref[...])
    nb = _rms_mod(h, gm, sh_ref[...]).astype(BF16)
    acc = None
    for c in range(w1_ref.shape[1] // FF_CHUNK):
        cols = slice(c * FF_CHUNK, (c + 1) * FF_CHUNK)
        a = jnp.maximum(_dot(nb, w1_ref[:, cols].astype(BF16)), 0.0)
        p = _dot((a * a).astype(BF16), w2_ref[cols, :].astype(BF16))
        acc = p if acc is None else acc + p
    out = h + g2_ref[...] * acc
    if final:
        ms = jnp.mean(out * out, axis=-1, keepdims=True)
        out = out * lax.rsqrt(ms + EPS) * fg_ref[...]
    return out


def _layer0_kernel(x_ref, rt_ref, ct_ref, y2_ref, vb_ref, vbp_ref, vbn_ref, wglu_ref, cw_ref, cb_ref, lng_ref,
                   lnb_ref, wout_ref, g1_ref, g_ref, sh_ref, sc_ref, g2_ref, w1_ref, w2_ref, o_ref,
                   h_s, ext_s, cat_s, *ys_s):
    step = pl.program_id(0)
    last = pl.num_programs(0) - 2
    i = jnp.minimum(step, last)
    nseg, rseg, _ = x_ref.shape

    @pl.when(step == 0)
    def _():
        h_s[...] = jnp.zeros(h_s.shape, F32)

    out = _mlp_stage(h_s[...], g_ref, sh_ref, sc_ref, g2_ref, w1_ref, w2_ref, None, False)
    for seg in range(nseg):
        o_ref[seg] = out[seg * rseg:(seg + 1) * rseg, :]
    _mixer0_stage(i, last, x_ref, rt_ref, ct_ref, y2_ref, vb_ref, vbp_ref, vbn_ref, wglu_ref, cw_ref, cb_ref,
                  lng_ref, lnb_ref, wout_ref, g1_ref, h_s, ext_s, cat_s, ys_s)


def _layer0(x3, tabs, y2, vb3, wglu, conv_w, conv_b, ln_g, ln_b, wout, g1, g, sh, sc, g2, w1, w2, *, layer):
    nseg, rps, d = x3.shape
    ngroups, _, kc = y2.shape
    s5w = ngroups * S5_H
    cw = vb3.shape[2]
    rseg = SLAB
    tm = nseg * rseg
    nt = rps // rseg
    rt, ct = tabs
    hb = rseg // CONV_HALO
    nhb = rps // CONV_HALO
    cur = lambda s: jnp.minimum(s, nt - 1)
    vec = lambda a: pl.BlockSpec(a.shape, lambda s: (0,) * a.ndim)
    tile = lambda w: pl.BlockSpec((nseg, rseg, w), lambda s: (0, cur(s), 0))
    slab = lambda w: pl.BlockSpec((None,) + w.shape[1:], lambda s: (layer, 0, 0), pipeline_mode=pl.Buffered(1))
    return pl.pallas_call(
        _layer0_kernel,
        grid=(nt + 1,),
        in_specs=[
            tile(d),
            pl.BlockSpec((None, nseg, rt.shape[2]), lambda s: (cur(s), 0, 0)),
            _resident(ct),
            pl.BlockSpec((ngroups, tm // T_CHUNK, kc), lambda s: (0, cur(s), 0)),
            tile(cw),
            pl.BlockSpec((nseg, CONV_HALO, cw), lambda s: (0, (cur(s) * hb + nhb - 1) % nhb, 0)),
            pl.BlockSpec((nseg, CONV_HALO, cw), lambda s: (0, ((cur(s) + 1) * hb) % nhb, 0)),
            _resident(wglu), vec(conv_w), vec(conv_b), vec(ln_g), vec(ln_b), _resident(wout), vec(g1),
            vec(g), vec(sh), vec(sc), vec(g2), slab(w1), slab(w2),
        ],
        out_specs=pl.BlockSpec((nseg, rseg, d), lambda s: (0, jnp.maximum(s - 1, 0), 0)),
        out_shape=jax.ShapeDtypeStruct((nseg, rps, d), F32),
        scratch_shapes=[pltpu.VMEM((tm, d), F32), pltpu.VMEM((rseg + 2 * CONV_HALO, cw), F32),
                        pltpu.VMEM((tm, d), BF16)] + [pltpu.VMEM((tm, LANES), F32)] * (s5w // LANES),
        compiler_params=_cparams(("arbitrary",)),
        name="layer0_tail",
    )(x3, rt, ct, y2, vb3, vb3, vb3, wglu, conv_w, conv_b, ln_g, ln_b, wout, g1, g, sh, sc, g2, w1, w2)


POOL_HALO = 8


def _pool_stage(i, last, h_ref, hp_ref, hn_ref, g_ref, sh_ref, sc_ref, g1_ref, pw_ref, pb_ref, ps_ref, h_dst, ext_s,
                tm, seq_len):
    gm = g_ref[...] * (1.0 + sc_ref[...])
    sh = sh_ref[...]
    ext_s[0:POOL_HALO, :] = jnp.where(i > 0, _rms_mod(hp_ref[...], gm, sh), 0.0)
    ext_s[POOL_HALO:POOL_HALO + tm, :] = _rms_mod(h_ref[...], gm, sh)
    ext_s[POOL_HALO + tm:, :] = jnp.where(i < last, _rms_mod(hn_ref[...], gm, sh), 0.0)
    t = i * tm + lax.broadcasted_iota(jnp.int32, (tm, 1), 0)
    pc = pw_ref.shape[1]
    for gi, win in enumerate(POOL_WINDOWS):
        cols = slice(gi * pc, (gi + 1) * pc)
        left = win // 2
        right = win - 1 - left
        s = None
        for off in range(-left, right + 1):
            v = ext_s[POOL_HALO + off:POOL_HALO + off + tm, cols]
            s = v if s is None else s + v
        lo = jnp.maximum(t - left, 0)
        hi = jnp.minimum(t + right, seq_len - 1)
        cnt = (hi - lo + 1).astype(F32)
        pg = s / cnt - ext_s[POOL_HALO:POOL_HALO + tm, cols]
        y = (_dot(pg.astype(BF16), pw_ref[gi].astype(BF16)) + pb_ref[:, cols]) * ps_ref[:, cols]
        h_dst[:, cols] = h_ref[:, cols] + g1_ref[:, cols] * y


def _layer1_kernel(h_ref, hp_ref, hn_ref, gx_ref, shx_ref, scx_ref, g1_ref, pw_ref, pb_ref, ps_ref,
                   g_ref, sh_ref, sc_ref, g2_ref, w1_ref, w2_ref, fg_ref, o_ref, h_s, ext_s, *, tm, seq_len):
    step = pl.program_id(0)
    last = pl.num_programs(0) - 2
    i = jnp.minimum(step, last)

    @pl.when(step == 0)
    def _():
        h_s[...] = jnp.zeros(h_s.shape, F32)

    o_ref[...] = _mlp_stage(h_s[...], g_ref, sh_ref, sc_ref, g2_ref, w1_ref, w2_ref, fg_ref, True)
    _pool_stage(i, last, h_ref, hp_ref, hn_ref, gx_ref, shx_ref, scx_ref, g1_ref, pw_ref, pb_ref, ps_ref,
                h_s, ext_s, tm, seq_len)


def _layer1(h, gx, shx, scx, g1, pool_w, pool_b, pool_scale, g, sh, sc, g2, w1, w2, fg, *, layer, tm):
    rows, d = h.shape
    nt = rows // tm
    hb = tm // POOL_HALO
    nhb = rows // POOL_HALO
    cur = lambda s: jnp.minimum(s, nt - 1)
    vec = pl.BlockSpec((1, d), lambda s: (0, 0))
    slab = lambda w: pl.BlockSpec((None,) + w.shape[1:], lambda s: (layer, 0, 0), pipeline_mode=pl.Buffered(1))
    return pl.pallas_call(
        functools.partial(_layer1_kernel, tm=tm, seq_len=rows),
        grid=(nt + 1,),
        in_specs=[
            pl.BlockSpec((tm, d), lambda s: (cur(s), 0)),
            pl.BlockSpec((POOL_HALO, d), lambda s: (jnp.maximum(cur(s) * hb - 1, 0), 0)),
            pl.BlockSpec((POOL_HALO, d), lambda s: (jnp.minimum((cur(s) + 1) * hb, nhb - 1), 0)),
            vec, vec, vec, vec, _resident(pool_w), vec, vec,
            vec, vec, vec, vec, slab(w1), slab(w2), vec,
        ],
        out_specs=pl.BlockSpec((tm, d), lambda s: (jnp.maximum(s - 1, 0), 0)),
        out_shape=jax.ShapeDtypeStruct((rows, d), F32),
        scratch_shapes=[pltpu.VMEM((tm, d), F32), pltpu.VMEM((tm + 2 * POOL_HALO, d), F32)],
        compiler_params=_cparams(("arbitrary",)),
        name="layer1",
    )(h, h, h, gx, shx, scx, g1, pool_w, pool_b, pool_scale, g, sh, sc, g2, w1, w2, fg)


def _pos_tables(rows, d, nseg):
    quarter = d // 4
    omega = 1.0 / (10000.0 ** (jnp.arange(quarter, dtype=F32) / quarter))

    def emb(n):
        ang = jnp.arange(n, dtype=F32)[:, None] * omega[None, :]
        return jnp.concatenate([jnp.sin(ang), jnp.cos(ang)], axis=-1)

    rt = jnp.swapaxes(emb(rows).reshape(nseg, rows // nseg, 2 * quarter), 0, 1)
    return rt, emb(GRID_W)


def _s5_params(lam_re, lam_im, log_step, b_re, b_im, c_re, c_im, d_skip, cps):
    ngroups, npole = lam_re.shape[1:]
    t = T_CHUNK
    lr = jnp.minimum(lam_re.astype(F32), LAMBDA_RE_MAX)
    li = lam_im.astype(F32)
    dt = jnp.exp(log_step.astype(F32))[..., None]
    ldr, ldi = lr * dt, li * dt

    def cpow(n):
        mag = jnp.exp(ldr * n)
        return mag * jnp.cos(ldi * n), mag * jnp.sin(ldi * n)

    lbr, lbi = cpow(1.0)
    den = lr * lr + li * li
    qr = ((lbr - 1.0) * lr + lbi * li) / den
    qi = (lbi * lr - (lbr - 1.0) * li) / den
    br, bi = b_re.astype(F32), b_im.astype(F32)
    bbr = qr[..., None] * br - qi[..., None] * bi
    bbi = qr[..., None] * bi + qi[..., None] * br
    cr = jnp.swapaxes(c_re.astype(F32), 2, 3)
    ci = jnp.swapaxes(c_im.astype(F32), 2, 3)

    pows = [cpow(float(2 ** b)) for b in range(4)]
    pow_re = jnp.stack([p[0] for p in pows], axis=-1)
    pow_im = jnp.stack([p[1] for p in pows], axis=-1)
    fill = jnp.zeros(bbr.shape[:3] + (LANES - 4 * S5_H - 8,), F32)
    pblk = jnp.concatenate([bbr, bbi, cr, ci, pow_re, pow_im, fill], axis=-1)
    pblk = jnp.moveaxis(pblk, 0, 1).reshape(ngroups, 2 * npole, LANES)

    lt = jnp.concatenate([jnp.swapaxes(bbr, 2, 3), -jnp.swapaxes(bbi, 2, 3)], axis=-1)
    lt = jnp.moveaxis(lt, 0, 1).reshape(ngroups, 2 * S5_H, 2 * npole)

    dvec = jnp.zeros((ngroups, SUBLANES, LANES), F32).at[:, 0, :S5_H].set(d_skip.astype(F32).reshape(ngroups, S5_H))

    atr, ati = cpow(float(t))
    asr, asi = cpow(float(t * cps))
    cat2 = lambda v: jnp.concatenate([v[0], v[1]], axis=-1)
    avec = jnp.stack([cat2(atr), cat2(ati), cat2(asr), cat2(asi)], axis=1)
    avec = jnp.concatenate([avec, jnp.zeros((ngroups, SUBLANES - 4, 2 * npole), F32)], axis=1)
    return pblk, lt, dvec, avec


def kernel(x, c, ctx, c_ctx, w_ada, b_ada, norm_mix_g, norm_mlp_g, w_in, w_out, s5_lam_re, s5_lam_im, s5_log_step, s5_b_re, s5_b_im, s5_c_re, s5_c_im, s5_d, s5_w_glu, conv_w, conv_b, conv_ln_g, conv_ln_b, pool_w, pool_b, pool_scale, mlp_w1, mlp_w2, final_g):
    bsz, seq, d = x.shape
    assert bsz == 1 and c.shape[0] == 1 and ctx.shape[0] == 1
    assert w_ada.shape[0] == 2, "the kernels implement the depth-2 block (one even layer, one odd layer)"
    s5w = s5_d.shape[1]
    rps = seq // SEG
    cps = rps // T_CHUNK
    tm = 512

    x3 = x.reshape(SEG, rps, d)
    tabs = _pos_tables(seq // GRID_W, d, SEG)

    cc = jnp.zeros((16, d), F32).at[0].set(c[0].astype(F32)).at[1].set(c_ctx.astype(F32))
    mods = _ada(cc, w_ada, b_ada)

    def mod(layer, row, which):
        return lax.slice(mods[layer], (row, which * d), (row + 1, (which + 1) * d))

    row_vec = lambda v: v.reshape(1, -1).astype(F32)

    pblk, lt, dvec, avec = _s5_params(s5_lam_re[0], s5_lam_im[0], s5_log_step[0], s5_b_re[0], s5_b_im[0],
                                      s5_c_re[0], s5_c_im[0], s5_d[0], cps)
    wsum_t, cmat, mfb = _s5ops(pblk, lt, dvec, gb=4)
    g_mix0 = row_vec(norm_mix_g[0])
    u2, vb3 = _inproj(x3, tabs, g_mix0, mod(0, 0, 0), mod(0, 0, 1), w_in[0], s5w=s5w)
    u2c, = _inproj(ctx, None, g_mix0, mod(0, 1, 0), mod(0, 1, 1), w_in[0], s5w=s5w)
    y2 = _s5core(u2, u2c, wsum_t, mfb, cmat, avec, gb=4, cps=cps)
    h3 = _layer0(x3, tabs, y2, vb3, s5_w_glu[0], conv_w[0].astype(F32), row_vec(conv_b[0]),
                 row_vec(conv_ln_g[0]), row_vec(conv_ln_b[0]), w_out[0], mod(0, 0, 2),
                 row_vec(norm_mlp_g[0]), mod(0, 0, 3), mod(0, 0, 4), mod(0, 0, 5), mlp_w1, mlp_w2, layer=0)
    h = h3.reshape(seq, d)

    h = _layer1(h, row_vec(norm_mix_g[1]), mod(1, 0, 0), mod(1, 0, 1), mod(1, 0, 2),
                pool_w[0], row_vec(pool_b[0]), row_vec(pool_scale[0]),
                row_vec(norm_mlp_g[1]), mod(1, 0, 3), mod(1, 0, 4), mod(1, 0, 5), mlp_w1, mlp_w2,
                row_vec(final_g), layer=1, tm=tm)
    return h[None]
```

```python
import functools

import jax
import jax.numpy as jnp
from jax import lax
from jax.experimental import pallas as pl
from jax.experimental.pallas import tpu as pltpu

GRID_W = 64
S5_H = 16
CONV_K = 31
POOL_WINDOWS = (2, 4, 8, 16)
EPS = 1e-6
LAMBDA_RE_MAX = -1e-4

LANES = 128
SUBLANES = 8
T_CHUNK = 16
SEG = SUBLANES
SLAB = GRID_W
GROUPS_PER_LANEBLOCK = LANES // S5_H
VMEM_LIMIT = 60 * 1024 * 1024

F32 = jnp.float32
BF16 = jnp.bfloat16


def _dot(a, b):
    return jnp.dot(a, b, preferred_element_type=F32)


def _dot_nt(a, b):
    return lax.dot_general(a, b, (((1,), (1,)), ((), ())), preferred_element_type=F32)


def _cparams(sem):
    return pltpu.CompilerParams(dimension_semantics=sem, vmem_limit_bytes=VMEM_LIMIT)


def _resident(a):
    return pl.BlockSpec(a.shape, lambda *_: (0,) * a.ndim, pipeline_mode=pl.Buffered(1))


def _ada_kernel(cc_ref, w_ref, b_ref, o_ref):
    cc = cc_ref[...]
    s = cc * jax.nn.sigmoid(cc)
    s_hi = s.astype(BF16)
    s_lo = (s - s_hi.astype(F32)).astype(BF16)
    w = w_ref[...]
    w_hi = w.astype(BF16)
    w_lo = (w - w_hi.astype(F32)).astype(BF16)
    o_ref[...] = _dot(s_hi, w_hi) + _dot(s_lo, w_hi) + _dot(s_hi, w_lo) + b_ref[...]


def _ada(cc, w_ada, b_ada):
    depth, d, n = w_ada.shape
    tn = 1536
    rows = cc.shape[0]
    return pl.pallas_call(
        _ada_kernel,
        grid=(depth, n // tn),
        in_specs=[
            pl.BlockSpec((rows, d), lambda l, j: (0, 0)),
            pl.BlockSpec((None, d, tn), lambda l, j: (l, 0, j)),
            pl.BlockSpec((None, 1, tn), lambda l, j: (l, 0, j)),
        ],
        out_specs=pl.BlockSpec((None, rows, tn), lambda l, j: (l, 0, j)),
        out_shape=jax.ShapeDtypeStruct((depth, rows, n), F32),
        compiler_params=_cparams(("arbitrary", "arbitrary")),
        name="ada_mod",
    )(cc, w_ada, b_ada.reshape(depth, 1, n))


def _slot_ids(nrows):
    return lax.broadcasted_iota(jnp.int32, (nrows, LANES), 1) // S5_H


def _to_chunked(zs_refs, nblk, nchunk, stride):
    slot = _slot_ids(nchunk)
    halves = T_CHUNK // GROUPS_PER_LANEBLOCK
    outs = {}
    for j, zs_ref in enumerate(zs_refs):
        for b in range(nblk):
            for m in range(halves):
                pieces = [zs_ref[pl.ds(b * T_CHUNK + m * 8 + tl, nchunk, stride=stride), :] for tl in range(8)]
                for i in range(8):
                    acc = None
                    for tl in range(8):
                        sh = ((tl - i) % 8) * S5_H
                        r = pieces[tl] if sh == 0 else pltpu.roll(pieces[tl], sh, 1)
                        acc = r if acc is None else jnp.where(slot == tl, r, acc)
                    outs[(8 * j + i, b, m)] = acc
    res = []
    for g in range(len(zs_refs) * GROUPS_PER_LANEBLOCK):
        rows = [jnp.concatenate([outs[(g, b, m)] for m in range(halves)], axis=1) for b in range(nblk)]
        res.append(rows[0] if nblk == 1 else jnp.concatenate(rows, axis=0))
    return res


def _from_chunked(y2_ref, ys_refs, nblk, nchunk, stride):
    slot = _slot_ids(nchunk)
    halves = T_CHUNK // GROUPS_PER_LANEBLOCK
    for j, ys_ref in enumerate(ys_refs):
        for b in range(nblk):
            for m in range(halves):
                srcs = [y2_ref[8 * j + i, b * nchunk:(b + 1) * nchunk, m * LANES:(m + 1) * LANES] for i in range(8)]
                for tl in range(8):
                    acc = None
                    for i in range(8):
                        sh = ((i - tl) % 8) * S5_H
                        r = srcs[i] if sh == 0 else pltpu.roll(srcs[i], sh, 1)
                        acc = r if acc is None else jnp.where(slot == i, r, acc)
                    ys_ref[pl.ds(b * T_CHUNK + m * 8 + tl, nchunk, stride=stride), :] = acc
            yield


def _add_pos(xs, rt_ref, ct_ref, seg):
    half = ct_ref.shape[1]
    pos = jnp.concatenate([jnp.broadcast_to(rt_ref[seg:seg + 1, :], (GRID_W, half)), ct_ref[...]], axis=1)
    return xs + pos


def _rms_mod(h, gm, sh):
    ms = jnp.mean(h * h, axis=-1, keepdims=True)
    return h * lax.rsqrt(ms + EPS) * gm + sh


def _inproj_kernel(*refs, latent, s5w):
    if latent:
        (x_ref, rt_ref, ct_ref, g_ref, sh_ref, sc_ref, w_ref, w1_ref, w2_ref,
         u2_ref, vb_ref, w1b_ref, w2b_ref, n_s, *z_s) = refs
        w1b_ref[...] = w1_ref[...].astype(BF16)
        w2b_ref[...] = w2_ref[...].astype(BF16)
    else:
        x_ref, g_ref, sh_ref, sc_ref, w_ref, u2_ref, n_s, *z_s = refs
    nseg, rseg, _ = x_ref.shape
    gm = g_ref[...] * (1.0 + sc_ref[...])
    sh = sh_ref[...]
    for seg in range(nseg):
        for r in range(rseg // GRID_W):
            xs = x_ref[seg, r * GRID_W:(r + 1) * GRID_W, :]
            if latent:
                xs = _add_pos(xs, rt_ref, ct_ref, seg)
            row0 = seg * rseg + r * GRID_W
            n_s[row0:row0 + GRID_W, :] = _rms_mod(xs, gm, sh).astype(BF16)
    z = _dot(n_s[...], w_ref[...].astype(BF16))
    for j, zs_ref in enumerate(z_s):
        zs_ref[...] = z[:, j * LANES:(j + 1) * LANES]
    if latent:
        cw = (z.shape[1] - s5w) // 2
        vb = z[:, s5w:s5w + cw] * jax.nn.sigmoid(z[:, s5w + cw:])
        for seg in range(nseg):
            vb_ref[seg] = vb[seg * rseg:(seg + 1) * rseg, :]
        chunks = _to_chunked(z_s, rseg // T_CHUNK, nseg, rseg)
    else:
        chunks = _to_chunked(z_s, 1, rseg // T_CHUNK, T_CHUNK)
    for g, ch in enumerate(chunks):
        u2_ref[g] = ch.astype(BF16)


def _cast_specs(w, nsteps, layer, step_of):
    rows, cols = w.shape[1:]
    rb = rows // nsteps
    return (pl.BlockSpec((None, rb, cols), lambda s: (layer, step_of(s), 0)),
            pl.BlockSpec((rb, cols), lambda s: (step_of(s), 0)),
            jax.ShapeDtypeStruct((rows, cols), BF16))


def _inproj(x3, tabs, g, sh, sc, w_in, mlp_w=None, *, s5w):
    nseg, rps, d = x3.shape
    n_in = w_in.shape[1]
    ngroups = s5w // S5_H
    kc = T_CHUNK * S5_H
    latent = tabs is not None
    rseg = SLAB if latent else rps
    tm = nseg * rseg
    cpt = tm // T_CHUNK
    vec = pl.BlockSpec((1, d), lambda i: (0, 0))
    in_specs = [pl.BlockSpec((nseg, rseg, d), lambda i: (0, i, 0))]
    args = [x3]
    if latent:
        rt, ct = tabs
        in_specs += [pl.BlockSpec((None, nseg, rt.shape[2]), lambda i: (i, 0, 0)), _resident(ct)]
        args += [rt, ct]
    in_specs += [vec, vec, vec, _resident(w_in)]
    args += [g, sh, sc, w_in]
    out_specs = [pl.BlockSpec((ngroups, cpt, kc), lambda i: (0, i, 0))]
    out_shape = [jax.ShapeDtypeStruct((ngroups, nseg * rps // T_CHUNK, kc), BF16)]
    if latent:
        cw = (n_in - s5w) // 2
        out_specs.append(pl.BlockSpec((nseg, rseg, cw), lambda i: (0, i, 0)))
        out_shape.append(jax.ShapeDtypeStruct((nseg, rps, cw), F32))
        for w in mlp_w:
            w_in_spec, w_out_spec, w_out_shape = _cast_specs(w, rps // rseg, 0, lambda s: s)
            in_specs.append(w_in_spec)
            args.append(w)
            out_specs.append(w_out_spec)
            out_shape.append(w_out_shape)
    return pl.pallas_call(
        functools.partial(_inproj_kernel, latent=latent, s5w=s5w),
        grid=(rps // rseg,),
        in_specs=in_specs,
        out_specs=out_specs,
        out_shape=out_shape,
        scratch_shapes=[pltpu.VMEM((tm, d), BF16)] + [pltpu.VMEM((tm, LANES), F32)] * (s5w // LANES),
        compiler_params=_cparams(("arbitrary",)),
        name="in_proj" if latent else "in_proj_ctx",
    )(*args)


def _cmul_add(ar, ai, sr, si, zr, zi):
    return ar * sr - ai * si + zr, ar * si + ai * sr + zi


def _shift_right(lo, hi, s):
    lane = lax.broadcasted_iota(jnp.int32, lo.shape, 1)
    zero = jnp.zeros_like(lo)
    if s == 0:
        return lo, hi
    if s >= LANES:
        r = s - LANES
        return zero, (lo if r == 0 else jnp.where(lane >= r, pltpu.roll(lo, r, 1), zero))
    rl, rh = pltpu.roll(lo, s, 1), pltpu.roll(hi, s, 1)
    return jnp.where(lane >= s, rl, zero), jnp.where(lane >= s, rh, rl)


def _shift_left(lo, hi, s):
    lane = lax.broadcasted_iota(jnp.int32, lo.shape, 1)
    zero = jnp.zeros_like(lo)
    if s == 0:
        return lo, hi
    if s >= LANES:
        r = s - LANES
        return (hi if r == 0 else jnp.where(lane < LANES - r, pltpu.roll(hi, LANES - r, 1), zero)), zero
    rl, rh = pltpu.roll(lo, LANES - s, 1), pltpu.roll(hi, LANES - s, 1)
    keep = lane < LANES - s
    return jnp.where(keep, rl, rh), jnp.where(keep, rh, zero)


def _cmul(a, b):
    return a[0] * b[0] - a[1] * b[1], a[0] * b[1] + a[1] * b[0]


def _split_bf16(v):
    hi = v.astype(BF16)
    return hi, (v - hi.astype(F32)).astype(BF16)


_P_BBR, _P_BBI, _P_CR, _P_CI, _P_POW_RE, _P_POW_IM = 0, 16, 32, 48, 64, 68


def _s5ops_kernel(p_ref, lt_ref, d_ref, wsum_ref, cmat_ref, mfb_ref, *, gb):
    npole2 = p_ref.shape[1]
    npole = npole2 // 2
    shape = (npole2, LANES)
    lane = lax.broadcasted_iota(jnp.int32, shape, 1)
    fwd_rows = lax.broadcasted_iota(jnp.int32, shape, 0) < npole
    slot = lane // S5_H
    bits = [((slot >> b) & 1) == 1 for b in range(3)]
    one = jnp.ones(shape, F32)
    zero = jnp.zeros(shape, F32)
    sub16 = lax.broadcasted_iota(jnp.int32, (S5_H, LANES), 0)
    lane16 = lax.broadcasted_iota(jnp.int32, (S5_H, LANES), 1)

    def tile16(v):
        v = jnp.where(lax.broadcasted_iota(jnp.int32, v.shape, 1) < S5_H, v, 0.0)
        for sh in (S5_H, 2 * S5_H, 4 * S5_H):
            v = v + pltpu.roll(v, sh, 1)
        return v

    for q in range(gb):
        x = p_ref[q]
        col = lambda k: jnp.broadcast_to(x[:, k:k + 1], shape)
        lam = [(col(_P_POW_RE + b), col(_P_POW_IM + b)) for b in range(4)]
        field = lambda o: tile16(x if o == 0 else pltpu.roll(x, LANES - o, 1))
        bb = field(_P_BBR), field(_P_BBI)
        cc = field(_P_CR), field(_P_CI)

        def low_power(masks):
            acc = None
            for b in range(3):
                f = jnp.where(masks[b], lam[b][0], one), jnp.where(masks[b], lam[b][1], zero)
                acc = f if acc is None else _cmul(acc, f)
            return acc

        up = low_power(bits)
        down = low_power([jnp.logical_not(m) for m in bits])
        e_tau = [up, _cmul(up, lam[3])]
        e_rev = [_cmul(down, lam[3]), down]
        e_tau1 = [_cmul(e, lam[0]) for e in e_tau]
        e_rev1 = [_cmul(e, lam[0]) for e in e_rev]

        def pick(f, b):
            return jnp.where(fwd_rows, f[0], b[0]), jnp.where(fwd_rows, f[1], b[1])

        rk = []
        for hf in range(2):
            cols = slice(hf * LANES, (hf + 1) * LANES)
            sr, si = _cmul(pick(e_rev[hf], e_tau[hf]), bb)
            wsum_ref[q, 0:npole2, cols] = sr.astype(BF16)
            wsum_ref[q, npole2:, cols] = si.astype(BF16)
            rr, ri = _cmul(pick(e_tau1[hf], e_rev1[hf]), cc)
            cmat_ref[q, 0:npole2, cols] = rr.astype(BF16)
            cmat_ref[q, npole2:, cols] = (-ri).astype(BF16)
            rk.append(_cmul(pick(e_tau[hf], e_rev[hf]), cc))

        kts = []
        for dr in range(2):
            rows = slice(dr * npole, (dr + 1) * npole)
            rhs = jnp.concatenate([jnp.concatenate([rk[0][part][rows], rk[1][part][rows]], axis=1)
                                   for part in range(2)], axis=0)
            l_hi, l_lo = _split_bf16(lt_ref[q, dr * S5_H:(dr + 1) * S5_H, :])
            r_hi, r_lo = _split_bf16(rhs)
            kts.append(_dot(l_hi, r_hi) + _dot(l_lo, r_hi) + _dot(l_hi, r_lo))
        kf_lo, kf_hi = kts[0][:, 0:LANES], kts[0][:, LANES:]
        kb_lo, kb_hi = kts[1][:, 0:LANES], kts[1][:, LANES:]

        dsk = tile16(jnp.broadcast_to(d_ref[q, 0:1, :], (S5_H, LANES)))
        for ti in range(T_CHUNK):
            f_lo, f_hi = _shift_right(kf_lo, kf_hi, ti * S5_H)
            b_lo, b_hi = _shift_left(kb_lo, kb_hi, (T_CHUNK - 1 - ti) * S5_H)
            diag = jnp.where(lane16 == (ti % 8) * S5_H + sub16, dsk, 0.0)
            lo = f_lo + b_lo + (diag if ti < 8 else 0.0)
            hi = f_hi + b_hi + (diag if ti >= 8 else 0.0)
            mfb_ref[q, ti * S5_H:(ti + 1) * S5_H, :] = jnp.concatenate([lo, hi], axis=1).astype(BF16)


def _s5ops(pblk, lt, dvec, *, gb):
    ngroups, npole2, _ = pblk.shape
    kc = T_CHUNK * S5_H
    blk = lambda a: pl.BlockSpec((gb,) + a.shape[1:], lambda i: (i,) + (0,) * (a.ndim - 1))
    out = jax.ShapeDtypeStruct((ngroups, 2 * npole2, kc), BF16)
    return pl.pallas_call(
        functools.partial(_s5ops_kernel, gb=gb),
        grid=(ngroups // gb,),
        in_specs=[blk(pblk), blk(lt), blk(dvec)],
        out_specs=[blk(out), blk(out), blk(out)],
        out_shape=[out, out, out],
        compiler_params=_cparams(("arbitrary",)),
        name="s5_ops",
    )(pblk, lt, dvec)


def _s5core_kernel(u2_ref, u2c_ref, wsum_ref, mfb_ref, cmat_ref, a_ref, y2_ref, z_s, s_s, car_s, *, gb, cps, ncc):
    half = LANES // 2
    isf8 = lax.broadcasted_iota(jnp.int32, (SUBLANES, LANES), 1) < half
    isf1 = lax.broadcasted_iota(jnp.int32, (1, LANES), 1) < half

    for q in range(gb):
        z_s[q] = _dot_nt(u2_ref[q], wsum_ref[q])

    ctx_fin = []
    for q in range(gb):
        zc = _dot_nt(u2c_ref[q], wsum_ref[q])
        ar, ai = a_ref[q, 0:1, :], a_ref[q, 1:2, :]
        sr = jnp.zeros((1, LANES), F32)
        si = jnp.zeros((1, LANES), F32)
        for k in range(ncc):
            kb = ncc - 1 - k
            zr = jnp.where(isf1, zc[k:k + 1, 0:LANES], zc[kb:kb + 1, 0:LANES])
            zi = jnp.where(isf1, zc[k:k + 1, LANES:], zc[kb:kb + 1, LANES:])
            sr, si = _cmul_add(ar, ai, sr, si, zr, zi)
        ctx_fin.append((sr, si))

    def body1(k, carry):
        rf = pl.multiple_of(k * SUBLANES, SUBLANES)
        rb = pl.multiple_of((cps - 1 - k) * SUBLANES, SUBLANES)
        new = []
        for q in range(gb):
            sr, si = carry[q]
            s_s[q, pl.ds(rf, SUBLANES), 0:half] = sr[:, 0:half]
            s_s[q, pl.ds(rb, SUBLANES), half:LANES] = sr[:, half:]
            s_s[q, pl.ds(rf, SUBLANES), LANES:LANES + half] = si[:, 0:half]
            s_s[q, pl.ds(rb, SUBLANES), LANES + half:] = si[:, half:]
            zr = jnp.where(isf8, z_s[q, pl.ds(rf, SUBLANES), 0:LANES], z_s[q, pl.ds(rb, SUBLANES), 0:LANES])
            zi = jnp.where(isf8, z_s[q, pl.ds(rf, SUBLANES), LANES:], z_s[q, pl.ds(rb, SUBLANES), LANES:])
            new.append(_cmul_add(a_ref[q, 0:1, :], a_ref[q, 1:2, :], sr, si, zr, zi))
        return tuple(new)

    zero8 = jnp.zeros((SUBLANES, LANES), F32)
    fin = lax.fori_loop(0, cps, body1, tuple((zero8, zero8) for _ in range(gb)))

    for q in range(gb):
        fr, fi = fin[q]
        asr, asi = a_ref[q, 2:3, :], a_ref[q, 3:4, :]
        cr, ci = ctx_fin[q]
        car_s[q, 0, 0:1, 0:half] = cr[:, 0:half]
        car_s[q, 0, SEG - 1:SEG, half:] = cr[:, half:]
        car_s[q, 1, 0:1, 0:half] = ci[:, 0:half]
        car_s[q, 1, SEG - 1:SEG, half:] = ci[:, half:]
        for m in range(SEG - 1):
            mb = SEG - 1 - m
            zr = jnp.where(isf1, fr[m:m + 1, :], fr[mb:mb + 1, :])
            zi = jnp.where(isf1, fi[m:m + 1, :], fi[mb:mb + 1, :])
            cr, ci = _cmul_add(asr, asi, cr, ci, zr, zi)
            car_s[q, 0, m + 1:m + 2, 0:half] = cr[:, 0:half]
            car_s[q, 0, mb - 1:mb, half:] = cr[:, half:]
            car_s[q, 1, m + 1:m + 2, 0:half] = ci[:, 0:half]
            car_s[q, 1, mb - 1:mb, half:] = ci[:, half:]

    def body2(k, pw):
        rf = pl.multiple_of(k * SUBLANES, SUBLANES)
        rb = pl.multiple_of((cps - 1 - k) * SUBLANES, SUBLANES)
        new = []
        for q in range(gb):
            pr, pi = pw[q]
            cr, ci = car_s[q, 0], car_s[q, 1]
            tr = pr * cr - pi * ci
            ti = pr * ci + pi * cr
            s_s[q, pl.ds(rf, SUBLANES), 0:half] += tr[:, 0:half]
            s_s[q, pl.ds(rb, SUBLANES), half:LANES] += tr[:, half:]
            s_s[q, pl.ds(rf, SUBLANES), LANES:LANES + half] += ti[:, 0:half]
            s_s[q, pl.ds(rb, SUBLANES), LANES + half:] += ti[:, half:]
            ar, ai = a_ref[q, 0:1, :], a_ref[q, 1:2, :]
            new.append((pr * ar - pi * ai, pr * ai + pi * ar))
        return tuple(new)

    one8 = jnp.ones((SUBLANES, LANES), F32)
    lax.fori_loop(0, cps, body2, tuple((one8, zero8) for _ in range(gb)))

    for q in range(gb):
        y2_ref[q] = _dot(u2_ref[q], mfb_ref[q]) + _dot(s_s[q].astype(BF16), cmat_ref[q])


def _s5core(u2, u2c, wsum_t, mfb, cmat, avec, *, gb, cps):
    ngroups, nrows, kc = u2.shape
    ncc = u2c.shape[1]
    ns = wsum_t.shape[1]
    blk = lambda a: pl.BlockSpec((gb,) + a.shape[1:], lambda i: (i,) + (0,) * (a.ndim - 1))
    return pl.pallas_call(
        functools.partial(_s5core_kernel, gb=gb, cps=cps, ncc=ncc),
        grid=(ngroups // gb,),
        in_specs=[blk(u2), blk(u2c), blk(wsum_t), blk(mfb), blk(cmat), blk(avec)],
        out_specs=pl.BlockSpec((gb, nrows, kc), lambda i: (i, 0, 0)),
        out_shape=jax.ShapeDtypeStruct((ngroups, nrows, kc), F32),
        scratch_shapes=[pltpu.VMEM((gb, nrows, ns), F32), pltpu.VMEM((gb, nrows, ns), F32),
                        pltpu.VMEM((gb, 2, SUBLANES, LANES), F32)],
        compiler_params=_cparams(("arbitrary",)),
        name="s5_core",
    )(u2, u2c, wsum_t, mfb, cmat, avec)


CONV_HALO = 16


def _mixer0_stage(i, last, x_ref, rt_ref, ct_ref, y2_ref, vb_ref, vbp_ref, vbn_ref, wglu_ref, cw_ref, cb_ref,
                  lng_ref, lnb_ref, wout_ref, g1_ref, h_dst, ext_s, cat_s, ys_s):
    nseg, rseg, d = x_ref.shape
    tm = nseg * rseg
    s5w = len(ys_s) * LANES
    cwid = vb_ref.shape[2]
    nblk = rseg // T_CHUNK
    yield len(ys_s) * nblk + tm // ROW_BLOCK + nseg // 2 + d // MXU_COLS
    yield from _from_chunked(y2_ref, ys_s, nblk, nseg, rseg)

    wglu = wglu_ref[...].astype(BF16)
    for r in range(tm // ROW_BLOCK):
        rows = slice(r * ROW_BLOCK, (r + 1) * ROW_BLOCK)
        ya = jax.nn.gelu(jnp.concatenate([ys_ref[rows, :] for ys_ref in ys_s], axis=1))
        ya = ya * jax.nn.sigmoid(_dot(ya.astype(BF16), wglu))
        cat_s[rows, 0:s5w] = ya.astype(BF16)
        yield

    zeros_h = jnp.zeros((CONV_HALO, cwid), F32)
    pad = CONV_K // 2
    ext_rows = rseg + 2 * CONV_HALO
    nacc = rseg + SUBLANES
    for seg in range(nseg):
        ext = ext_s.at[seg % ext_s.shape[0]]
        prev_first = vbp_ref[seg - 1] if seg > 0 else zeros_h
        next_last = vbn_ref[seg + 1] if seg < nseg - 1 else zeros_h
        ext[0:CONV_HALO, :] = jnp.where(i > 0, vbp_ref[seg], prev_first)
        ext[CONV_HALO:CONV_HALO + rseg, :] = vb_ref[seg]
        ext[CONV_HALO + rseg:ext_rows, :] = jnp.where(i < last, vbn_ref[seg], next_last)
        acc = jnp.broadcast_to(cb_ref[...], (rseg, cwid))
        for s in range(SUBLANES):
            part = None
            for qq in range(ext_rows // SUBLANES):
                k = qq * SUBLANES + s - (CONV_HALO - pad)
                if k < 0 or k >= CONV_K or qq * SUBLANES + nacc > ext_rows:
                    continue
                term = cw_ref[k:k + 1, :] * ext[qq * SUBLANES:qq * SUBLANES + nacc, :]
                part = term if part is None else part + term
            acc = acc + part[s:s + rseg, :]
        mu = jnp.mean(acc, axis=-1, keepdims=True)
        xc = acc - mu
        var = jnp.mean(xc * xc, axis=-1, keepdims=True)
        y = xc * lax.rsqrt(var + EPS) * lng_ref[...] + lnb_ref[...]
        y = y * jax.nn.sigmoid(y)
        cat_s[seg * rseg:(seg + 1) * rseg, s5w:] = y.astype(BF16)
        if seg % 2 == 1:
            yield

    half = ct_ref.shape[1]
    for j in range(d // MXU_COLS):
        c0 = j * MXU_COLS
        cols = slice(c0, c0 + MXU_COLS)
        yx = _dot(cat_s[...], wout_ref[:, cols].astype(BF16))
        g1 = g1_ref[:, cols]
        for seg in range(nseg):
            rows = slice(seg * rseg, (seg + 1) * rseg)
            if c0 < half:
                pos = jnp.broadcast_to(rt_ref[seg:seg + 1, cols], (rseg, MXU_COLS))
            else:
                pos = ct_ref[:, c0 - half:c0 - half + MXU_COLS]
            h_dst[rows, cols] = x_ref[seg, :, cols] + pos + g1 * yx[rows, :]
        yield


FF_CHUNK = 1024


MXU_COLS = 256
ROW_BLOCK = 128


def _mlp_stage(h_src, g_ref, sh_ref, sc_ref, g2_ref, w1_ref, w2_ref, fg_ref, final, store, hc_s, nb_s, a_s, acc_s):
    tm, d = hc_s.shape
    dff = w1_ref.shape[1]
    nrb, nch, nsub = tm // ROW_BLOCK, dff // FF_CHUNK, FF_CHUNK // MXU_COLS
    yield 2 * nrb + nch * (nsub + d // MXU_COLS)
    gm = g_ref[...] * (1.0 + sc_ref[...])
    for r in range(nrb):
        rows = slice(r * ROW_BLOCK, (r + 1) * ROW_BLOCK)
        h = h_src[rows, :]
        hc_s[rows, :] = h
        nb_s[rows, :] = _rms_mod(h, gm, sh_ref[...]).astype(BF16)
        yield
    for c in range(nch):
        a_buf = a_s[c % len(a_s)]
        for j in range(nsub):
            col0 = c * FF_CHUNK + j * MXU_COLS
            a = jnp.maximum(_dot(nb_s[...], w1_ref[:, col0:col0 + MXU_COLS]), 0.0)
            a_buf[:, j * MXU_COLS:(j + 1) * MXU_COLS] = (a * a).astype(BF16)
            yield
        for j in range(d // MXU_COLS):
            cols = slice(j * MXU_COLS, (j + 1) * MXU_COLS)
            p = _dot(a_buf[...], w2_ref[c * FF_CHUNK:(c + 1) * FF_CHUNK, cols])
            if c == 0:
                acc_s[:, cols] = p
            else:
                acc_s[:, cols] += p
            yield
    for r in range(nrb):
        rows = slice(r * ROW_BLOCK, (r + 1) * ROW_BLOCK)
        out = hc_s[rows, :] + g2_ref[...] * acc_s[rows, :]
        if final:
            ms = jnp.mean(out * out, axis=-1, keepdims=True)
            out = out * lax.rsqrt(ms + EPS) * fg_ref[...]
        store(r, out)
        yield


def _mlp_scratch(tm, d):
    return [pltpu.VMEM((tm, d), F32), pltpu.VMEM((tm, d), BF16), pltpu.VMEM((tm, FF_CHUNK), BF16),
            pltpu.VMEM((tm, FF_CHUNK), BF16), pltpu.VMEM((tm, d), F32)]


def _interleave(*stages):
    counts = [next(st) for st in stages]
    done = [0] * len(stages)
    live = set(range(len(stages)))
    while live:
        k = min(live, key=lambda s: (done[s] + 0.5) / counts[s])
        try:
            next(stages[k])
            done[k] += 1
        except StopIteration:
            live.remove(k)


def _layer0_kernel(x_ref, rt_ref, ct_ref, y2_ref, vb_ref, vbp_ref, vbn_ref, wglu_ref, cw_ref, cb_ref, lng_ref,
                   lnb_ref, wout_ref, g1_ref, g_ref, sh_ref, sc_ref, g2_ref, w1_ref, w2_ref, w1n_ref, w2n_ref,
                   o_ref, w1nb_ref, w2nb_ref, h_s, ext_s, cat_s, hc_s, nb_s, a0_s, a1_s, acc_s, *ys_s):
    step = pl.program_id(0)
    last = pl.num_programs(0) - 2
    i = jnp.minimum(step, last)
    nseg, rseg, _ = x_ref.shape
    w1nb_ref[...] = w1n_ref[...].astype(BF16)
    w2nb_ref[...] = w2n_ref[...].astype(BF16)

    @pl.when(step == 0)
    def _():
        h_s[...] = jnp.zeros(h_s.shape, F32)

    def store(r, out):
        for k in range(ROW_BLOCK // rseg):
            o_ref[r * (ROW_BLOCK // rseg) + k] = out[k * rseg:(k + 1) * rseg, :]

    _interleave(
        _mlp_stage(h_s, g_ref, sh_ref, sc_ref, g2_ref, w1_ref, w2_ref, None, False, store,
                   hc_s, nb_s, (a0_s, a1_s), acc_s),
        _mixer0_stage(i, last, x_ref, rt_ref, ct_ref, y2_ref, vb_ref, vbp_ref, vbn_ref, wglu_ref, cw_ref, cb_ref,
                      lng_ref, lnb_ref, wout_ref, g1_ref, h_s, ext_s, cat_s, ys_s))


def _layer0(x3, tabs, y2, vb3, wglu, conv_w, conv_b, ln_g, ln_b, wout, g1, g, sh, sc, g2, w1, w2, next_mlp_w,
            *, next_layer):
    nseg, rps, d = x3.shape
    ngroups, _, kc = y2.shape
    s5w = ngroups * S5_H
    cw = vb3.shape[2]
    rseg = SLAB
    tm = nseg * rseg
    nt = rps // rseg
    rt, ct = tabs
    hb = rseg // CONV_HALO
    nhb = rps // CONV_HALO
    cur = lambda s: jnp.minimum(s, nt - 1)
    vec = lambda a: pl.BlockSpec(a.shape, lambda s: (0,) * a.ndim)
    tile = lambda w: pl.BlockSpec((nseg, rseg, w), lambda s: (0, cur(s), 0))
    casts = [_cast_specs(w, nt, next_layer, cur) for w in next_mlp_w]
    return pl.pallas_call(
        _layer0_kernel,
        grid=(nt + 1,),
        in_specs=[
            tile(d),
            pl.BlockSpec((None, nseg, rt.shape[2]), lambda s: (cur(s), 0, 0)),
            _resident(ct),
            pl.BlockSpec((ngroups, tm // T_CHUNK, kc), lambda s: (0, cur(s), 0)),
            tile(cw),
            pl.BlockSpec((nseg, CONV_HALO, cw), lambda s: (0, (cur(s) * hb + nhb - 1) % nhb, 0)),
            pl.BlockSpec((nseg, CONV_HALO, cw), lambda s: (0, ((cur(s) + 1) * hb) % nhb, 0)),
            _resident(wglu), vec(conv_w), vec(conv_b), vec(ln_g), vec(ln_b), _resident(wout), vec(g1),
            vec(g), vec(sh), vec(sc), vec(g2), _resident(w1), _resident(w2), casts[0][0], casts[1][0],
        ],
        out_specs=[pl.BlockSpec((nseg, rseg, d), lambda s: (0, jnp.maximum(s - 1, 0), 0)),
                   casts[0][1], casts[1][1]],
        out_shape=[jax.ShapeDtypeStruct((nseg, rps, d), F32), casts[0][2], casts[1][2]],
        scratch_shapes=[pltpu.VMEM((tm, d), F32), pltpu.VMEM((2, rseg + 2 * CONV_HALO, cw), F32),
                        pltpu.VMEM((tm, d), BF16)] + _mlp_scratch(tm, d)
        + [pltpu.VMEM((tm, LANES), F32)] * (s5w // LANES),
        compiler_params=_cparams(("arbitrary",)),
        name="layer0_tail",
    )(x3, rt, ct, y2, vb3, vb3, vb3, wglu, conv_w, conv_b, ln_g, ln_b, wout, g1, g, sh, sc, g2, w1, w2,
      *next_mlp_w)


POOL_HALO = 8


def _pool_stage(i, last, h_ref, hp_ref, hn_ref, g_ref, sh_ref, sc_ref, g1_ref, pw_ref, pb_ref, ps_ref, h_dst, ext_s,
                tm, seq_len):
    nrb = tm // ROW_BLOCK
    yield 1 + nrb + len(POOL_WINDOWS) * nrb
    gm = g_ref[...] * (1.0 + sc_ref[...])
    sh = sh_ref[...]
    ext_s[0:POOL_HALO, :] = jnp.where(i > 0, _rms_mod(hp_ref[...], gm, sh), 0.0)
    ext_s[POOL_HALO + tm:, :] = jnp.where(i < last, _rms_mod(hn_ref[...], gm, sh), 0.0)
    yield
    for r in range(nrb):
        rows = slice(r * ROW_BLOCK, (r + 1) * ROW_BLOCK)
        ext_s[POOL_HALO + r * ROW_BLOCK:POOL_HALO + (r + 1) * ROW_BLOCK, :] = _rms_mod(h_ref[rows, :], gm, sh)
        yield
    pc = pw_ref.shape[1]
    for gi, win in enumerate(POOL_WINDOWS):
        cols = slice(gi * pc, (gi + 1) * pc)
        left = win // 2
        right = win - 1 - left
        wgt = pw_ref[gi].astype(BF16)
        for r in range(nrb):
            rows = slice(r * ROW_BLOCK, (r + 1) * ROW_BLOCK)
            row0 = POOL_HALO + r * ROW_BLOCK
            s = None
            for off in range(-left, right + 1):
                v = ext_s[row0 + off:row0 + off + ROW_BLOCK, cols]
                s = v if s is None else s + v
            t = i * tm + r * ROW_BLOCK + lax.broadcasted_iota(jnp.int32, (ROW_BLOCK, 1), 0)
            lo = jnp.maximum(t - left, 0)
            hi = jnp.minimum(t + right, seq_len - 1)
            cnt = (hi - lo + 1).astype(F32)
            pg = s / cnt - ext_s[row0:row0 + ROW_BLOCK, cols]
            y = (_dot(pg.astype(BF16), wgt) + pb_ref[:, cols]) * ps_ref[:, cols]
            h_dst[rows, cols] = h_ref[rows, cols] + g1_ref[:, cols] * y
            yield


def _layer1_kernel(h_ref, hp_ref, hn_ref, gx_ref, shx_ref, scx_ref, g1_ref, pw_ref, pb_ref, ps_ref,
                   g_ref, sh_ref, sc_ref, g2_ref, w1_ref, w2_ref, fg_ref, o_ref, h_s, ext_s,
                   hc_s, nb_s, a0_s, a1_s, acc_s, *, tm, seq_len):
    step = pl.program_id(0)
    last = pl.num_programs(0) - 2
    i = jnp.minimum(step, last)

    @pl.when(step == 0)
    def _():
        h_s[...] = jnp.zeros(h_s.shape, F32)

    def store(r, out):
        o_ref[r * ROW_BLOCK:(r + 1) * ROW_BLOCK, :] = out

    _interleave(
        _mlp_stage(h_s, g_ref, sh_ref, sc_ref, g2_ref, w1_ref, w2_ref, fg_ref, True, store,
                   hc_s, nb_s, (a0_s, a1_s), acc_s),
        _pool_stage(i, last, h_ref, hp_ref, hn_ref, gx_ref, shx_ref, scx_ref, g1_ref, pw_ref, pb_ref, ps_ref,
                    h_s, ext_s, tm, seq_len))


def _layer1(h, gx, shx, scx, g1, pool_w, pool_b, pool_scale, g, sh, sc, g2, w1, w2, fg, *, tm):
    rows, d = h.shape
    nt = rows // tm
    hb = tm // POOL_HALO
    nhb = rows // POOL_HALO
    cur = lambda s: jnp.minimum(s, nt - 1)
    vec = pl.BlockSpec((1, d), lambda s: (0, 0))
    return pl.pallas_call(
        functools.partial(_layer1_kernel, tm=tm, seq_len=rows),
        grid=(nt + 1,),
        in_specs=[
            pl.BlockSpec((tm, d), lambda s: (cur(s), 0)),
            pl.BlockSpec((POOL_HALO, d), lambda s: (jnp.maximum(cur(s) * hb - 1, 0), 0)),
            pl.BlockSpec((POOL_HALO, d), lambda s: (jnp.minimum((cur(s) + 1) * hb, nhb - 1), 0)),
            vec, vec, vec, vec, _resident(pool_w), vec, vec,
            vec, vec, vec, vec, _resident(w1), _resident(w2), vec,
        ],
        out_specs=pl.BlockSpec((tm, d), lambda s: (jnp.maximum(s - 1, 0), 0)),
        out_shape=jax.ShapeDtypeStruct((rows, d), F32),
        scratch_shapes=[pltpu.VMEM((tm, d), F32), pltpu.VMEM((tm + 2 * POOL_HALO, d), F32)] + _mlp_scratch(tm, d),
        compiler_params=_cparams(("arbitrary",)),
        name="layer1",
    )(h, h, h, gx, shx, scx, g1, pool_w, pool_b, pool_scale, g, sh, sc, g2, w1, w2, fg)


def _pos_tables(rows, d, nseg):
    quarter = d // 4
    omega = 1.0 / (10000.0 ** (jnp.arange(quarter, dtype=F32) / quarter))

    def emb(n):
        ang = jnp.arange(n, dtype=F32)[:, None] * omega[None, :]
        return jnp.concatenate([jnp.sin(ang), jnp.cos(ang)], axis=-1)

    rt = jnp.swapaxes(emb(rows).reshape(nseg, rows // nseg, 2 * quarter), 0, 1)
    return rt, emb(GRID_W)


def _s5_params(lam_re, lam_im, log_step, b_re, b_im, c_re, c_im, d_skip, cps):
    ngroups, npole = lam_re.shape[1:]
    t = T_CHUNK
    lr = jnp.minimum(lam_re.astype(F32), LAMBDA_RE_MAX)
    li = lam_im.astype(F32)
    dt = jnp.exp(log_step.astype(F32))[..., None]
    ldr, ldi = lr * dt, li * dt

    def cpow(n):
        mag = jnp.exp(ldr * n)
        return mag * jnp.cos(ldi * n), mag * jnp.sin(ldi * n)

    lbr, lbi = cpow(1.0)
    den = lr * lr + li * li
    qr = ((lbr - 1.0) * lr + lbi * li) / den
    qi = (lbi * lr - (lbr - 1.0) * li) / den
    br, bi = b_re.astype(F32), b_im.astype(F32)
    bbr = qr[..., None] * br - qi[..., None] * bi
    bbi = qr[..., None] * bi + qi[..., None] * br
    cr = jnp.swapaxes(c_re.astype(F32), 2, 3)
    ci = jnp.swapaxes(c_im.astype(F32), 2, 3)

    pows = [cpow(float(2 ** b)) for b in range(4)]
    pow_re = jnp.stack([p[0] for p in pows], axis=-1)
    pow_im = jnp.stack([p[1] for p in pows], axis=-1)
    fill = jnp.zeros(bbr.shape[:3] + (LANES - 4 * S5_H - 8,), F32)
    pblk = jnp.concatenate([bbr, bbi, cr, ci, pow_re, pow_im, fill], axis=-1)
    pblk = jnp.moveaxis(pblk, 0, 1).reshape(ngroups, 2 * npole, LANES)

    lt = jnp.concatenate([jnp.swapaxes(bbr, 2, 3), -jnp.swapaxes(bbi, 2, 3)], axis=-1)
    lt = jnp.moveaxis(lt, 0, 1).reshape(ngroups, 2 * S5_H, 2 * npole)

    dvec = jnp.zeros((ngroups, SUBLANES, LANES), F32).at[:, 0, :S5_H].set(d_skip.astype(F32).reshape(ngroups, S5_H))

    atr, ati = cpow(float(t))
    asr, asi = cpow(float(t * cps))
    cat2 = lambda v: jnp.concatenate([v[0], v[1]], axis=-1)
    avec = jnp.stack([cat2(atr), cat2(ati), cat2(asr), cat2(asi)], axis=1)
    avec = jnp.concatenate([avec, jnp.zeros((ngroups, SUBLANES - 4, 2 * npole), F32)], axis=1)
    return pblk, lt, dvec, avec


def kernel(x, c, ctx, c_ctx, w_ada, b_ada, norm_mix_g, norm_mlp_g, w_in, w_out, s5_lam_re, s5_lam_im, s5_log_step, s5_b_re, s5_b_im, s5_c_re, s5_c_im, s5_d, s5_w_glu, conv_w, conv_b, conv_ln_g, conv_ln_b, pool_w, pool_b, pool_scale, mlp_w1, mlp_w2, final_g):
    bsz, seq, d = x.shape
    assert bsz == 1 and c.shape[0] == 1 and ctx.shape[0] == 1
    assert w_ada.shape[0] == 2, "the kernels implement the depth-2 block (one even layer, one odd layer)"
    s5w = s5_d.shape[1]
    rps = seq // SEG
    cps = rps // T_CHUNK
    tm = 512

    x3 = x.reshape(SEG, rps, d)
    tabs = _pos_tables(seq // GRID_W, d, SEG)

    cc = jnp.zeros((16, d), F32).at[0].set(c[0].astype(F32)).at[1].set(c_ctx.astype(F32))
    mods = _ada(cc, w_ada, b_ada)

    def mod(layer, row, which):
        return lax.slice(mods[layer], (row, which * d), (row + 1, (which + 1) * d))

    row_vec = lambda v: v.reshape(1, -1).astype(F32)

    pblk, lt, dvec, avec = _s5_params(s5_lam_re[0], s5_lam_im[0], s5_log_step[0], s5_b_re[0], s5_b_im[0],
                                      s5_c_re[0], s5_c_im[0], s5_d[0], cps)
    wsum_t, cmat, mfb = _s5ops(pblk, lt, dvec, gb=4)
    g_mix0 = row_vec(norm_mix_g[0])
    u2, vb3, w1_l0, w2_l0 = _inproj(x3, tabs, g_mix0, mod(0, 0, 0), mod(0, 0, 1), w_in[0], (mlp_w1, mlp_w2), s5w=s5w)
    u2c, = _inproj(ctx, None, g_mix0, mod(0, 1, 0), mod(0, 1, 1), w_in[0], s5w=s5w)
    y2 = _s5core(u2, u2c, wsum_t, mfb, cmat, avec, gb=4, cps=cps)
    h3, w1_l1, w2_l1 = _layer0(x3, tabs, y2, vb3, s5_w_glu[0], conv_w[0].astype(F32), row_vec(conv_b[0]),
                               row_vec(conv_ln_g[0]), row_vec(conv_ln_b[0]), w_out[0], mod(0, 0, 2),
                               row_vec(norm_mlp_g[0]), mod(0, 0, 3), mod(0, 0, 4), mod(0, 0, 5), w1_l0, w2_l0,
                               (mlp_w1, mlp_w2), next_layer=1)
    h = h3.reshape(seq, d)

    h = _layer1(h, row_vec(norm_mix_g[1]), mod(1, 0, 0), mod(1, 0, 1), mod(1, 0, 2),
                pool_w[0], row_vec(pool_b[0]), row_vec(pool_scale[0]),
                row_vec(norm_mlp_g[1]), mod(1, 0, 3), mod(1, 0, 4), mod(1, 0, 5), w1_l1, w2_l1,
                row_vec(final_g), tm=tm)
    return h[None]
```

```python
import functools

import jax
import jax.numpy as jnp
from jax import lax
from jax.experimental import pallas as pl
from jax.experimental.pallas import tpu as pltpu

GRID_W = 64
S5_H = 16
CONV_K = 31
POOL_WINDOWS = (2, 4, 8, 16)
EPS = 1e-6
LAMBDA_RE_MAX = -1e-4

LANES = 128
SUBLANES = 8
T_CHUNK = 16
SEG = SUBLANES
SLAB = GRID_W
GROUPS_PER_LANEBLOCK = LANES // S5_H
VMEM_LIMIT = 60 * 1024 * 1024

F32 = jnp.float32
BF16 = jnp.bfloat16


def _dot(a, b):
    return jnp.dot(a, b, preferred_element_type=F32)


def _dot_nt(a, b):
    return lax.dot_general(a, b, (((1,), (1,)), ((), ())), preferred_element_type=F32)


def _cparams(sem):
    return pltpu.CompilerParams(dimension_semantics=sem, vmem_limit_bytes=VMEM_LIMIT)


def _resident(a):
    return pl.BlockSpec(a.shape, lambda *_: (0,) * a.ndim, pipeline_mode=pl.Buffered(1))


def _ada_kernel(cc_ref, w_ref, b_ref, o_ref):
    cc = cc_ref[...]
    s = cc * jax.nn.sigmoid(cc)
    s_hi = s.astype(BF16)
    s_lo = (s - s_hi.astype(F32)).astype(BF16)
    w = w_ref[...]
    w_hi = w.astype(BF16)
    w_lo = (w - w_hi.astype(F32)).astype(BF16)
    o_ref[...] = _dot(s_hi, w_hi) + _dot(s_lo, w_hi) + _dot(s_hi, w_lo) + b_ref[...]


def _ada(cc, w_ada, b_ada):
    depth, d, n = w_ada.shape
    tn = 1536
    rows = cc.shape[0]
    return pl.pallas_call(
        _ada_kernel,
        grid=(depth, n // tn),
        in_specs=[
            pl.BlockSpec((rows, d), lambda l, j: (0, 0)),
            pl.BlockSpec((None, d, tn), lambda l, j: (l, 0, j)),
            pl.BlockSpec((None, 1, tn), lambda l, j: (l, 0, j)),
        ],
        out_specs=pl.BlockSpec((None, rows, tn), lambda l, j: (l, 0, j)),
        out_shape=jax.ShapeDtypeStruct((depth, rows, n), F32),
        compiler_params=_cparams(("arbitrary", "arbitrary")),
        name="ada_mod",
    )(cc, w_ada, b_ada.reshape(depth, 1, n))


def _slot_ids(nrows):
    return lax.broadcasted_iota(jnp.int32, (nrows, LANES), 1) // S5_H


def _to_chunked(zs_refs, nblk, nchunk, stride):
    slot = _slot_ids(nchunk)
    halves = T_CHUNK // GROUPS_PER_LANEBLOCK
    outs = {}
    for j, zs_ref in enumerate(zs_refs):
        for b in range(nblk):
            for m in range(halves):
                pieces = [zs_ref[pl.ds(b * T_CHUNK + m * 8 + tl, nchunk, stride=stride), :] for tl in range(8)]
                for i in range(8):
                    acc = None
                    for tl in range(8):
                        sh = ((tl - i) % 8) * S5_H
                        r = pieces[tl] if sh == 0 else pltpu.roll(pieces[tl], sh, 1)
                        acc = r if acc is None else jnp.where(slot == tl, r, acc)
                    outs[(8 * j + i, b, m)] = acc
    res = []
    for g in range(len(zs_refs) * GROUPS_PER_LANEBLOCK):
        rows = [jnp.concatenate([outs[(g, b, m)] for m in range(halves)], axis=1) for b in range(nblk)]
        res.append(rows[0] if nblk == 1 else jnp.concatenate(rows, axis=0))
    return res


def _from_chunked(y2_ref, ys_refs, nblk, nchunk, stride):
    slot = _slot_ids(nchunk)
    halves = T_CHUNK // GROUPS_PER_LANEBLOCK
    for j, ys_ref in enumerate(ys_refs):
        for b in range(nblk):
            for m in range(halves):
                srcs = [y2_ref[8 * j + i, b * nchunk:(b + 1) * nchunk, m * LANES:(m + 1) * LANES] for i in range(8)]
                for tl in range(8):
                    acc = None
                    for i in range(8):
                        sh = ((i - tl) % 8) * S5_H
                        r = srcs[i] if sh == 0 else pltpu.roll(srcs[i], sh, 1)
                        acc = r if acc is None else jnp.where(slot == i, r, acc)
                    ys_ref[pl.ds(b * T_CHUNK + m * 8 + tl, nchunk, stride=stride), :] = acc
            yield


def _add_pos(xs, rt_ref, ct_ref, seg):
    half = ct_ref.shape[1]
    pos = jnp.concatenate([jnp.broadcast_to(rt_ref[seg:seg + 1, :], (GRID_W, half)), ct_ref[...]], axis=1)
    return xs + pos


def _rms_mod(h, gm, sh):
    ms = jnp.mean(h * h, axis=-1, keepdims=True)
    return h * lax.rsqrt(ms + EPS) * gm + sh


def _inproj_kernel(*refs, latent, s5w):
    if latent:
        (x_ref, rt_ref, ct_ref, g_ref, sh_ref, sc_ref, w_ref, w1_ref, w2_ref,
         u2_ref, vb_ref, w1b_ref, w2b_ref, n_s, *z_s) = refs
        w1b_ref[...] = w1_ref[...].astype(BF16)
        w2b_ref[...] = w2_ref[...].astype(BF16)
    else:
        x_ref, g_ref, sh_ref, sc_ref, w_ref, u2_ref, n_s, *z_s = refs
    nseg, rseg, _ = x_ref.shape
    gm = g_ref[...] * (1.0 + sc_ref[...])
    sh = sh_ref[...]
    for seg in range(nseg):
        for r in range(rseg // GRID_W):
            xs = x_ref[seg, r * GRID_W:(r + 1) * GRID_W, :]
            if latent:
                xs = _add_pos(xs, rt_ref, ct_ref, seg)
            row0 = seg * rseg + r * GRID_W
            n_s[row0:row0 + GRID_W, :] = _rms_mod(xs, gm, sh).astype(BF16)
    z = _dot(n_s[...], w_ref[...].astype(BF16))
    for j, zs_ref in enumerate(z_s):
        zs_ref[...] = z[:, j * LANES:(j + 1) * LANES]
    if latent:
        cw = (z.shape[1] - s5w) // 2
        vb = z[:, s5w:s5w + cw] * jax.nn.sigmoid(z[:, s5w + cw:])
        for seg in range(nseg):
            vb_ref[seg] = vb[seg * rseg:(seg + 1) * rseg, :]
        chunks = _to_chunked(z_s, rseg // T_CHUNK, nseg, rseg)
    else:
        chunks = _to_chunked(z_s, 1, rseg // T_CHUNK, T_CHUNK)
    for g, ch in enumerate(chunks):
        u2_ref[g] = ch.astype(BF16)


def _cast_specs(w, nsteps, layer, step_of):
    rows, cols = w.shape[1:]
    rb = rows // nsteps
    return (pl.BlockSpec((None, rb, cols), lambda s: (layer, step_of(s), 0)),
            pl.BlockSpec((rb, cols), lambda s: (step_of(s), 0)),
            jax.ShapeDtypeStruct((rows, cols), BF16))


def _inproj(x3, tabs, g, sh, sc, w_in, mlp_w=None, *, s5w):
    nseg, rps, d = x3.shape
    n_in = w_in.shape[1]
    ngroups = s5w // S5_H
    kc = T_CHUNK * S5_H
    latent = tabs is not None
    rseg = SLAB if latent else rps
    tm = nseg * rseg
    cpt = tm // T_CHUNK
    vec = pl.BlockSpec((1, d), lambda i: (0, 0))
    in_specs = [pl.BlockSpec((nseg, rseg, d), lambda i: (0, i, 0))]
    args = [x3]
    if latent:
        rt, ct = tabs
        in_specs += [pl.BlockSpec((None, nseg, rt.shape[2]), lambda i: (i, 0, 0)), _resident(ct)]
        args += [rt, ct]
    in_specs += [vec, vec, vec, _resident(w_in)]
    args += [g, sh, sc, w_in]
    out_specs = [pl.BlockSpec((ngroups, cpt, kc), lambda i: (0, i, 0))]
    out_shape = [jax.ShapeDtypeStruct((ngroups, nseg * rps // T_CHUNK, kc), BF16)]
    if latent:
        cw = (n_in - s5w) // 2
        out_specs.append(pl.BlockSpec((nseg, rseg, cw), lambda i: (0, i, 0)))
        out_shape.append(jax.ShapeDtypeStruct((nseg, rps, cw), F32))
        for w in mlp_w:
            w_in_spec, w_out_spec, w_out_shape = _cast_specs(w, rps // rseg, 0, lambda s: s)
            in_specs.append(w_in_spec)
            args.append(w)
            out_specs.append(w_out_spec)
            out_shape.append(w_out_shape)
    return pl.pallas_call(
        functools.partial(_inproj_kernel, latent=latent, s5w=s5w),
        grid=(rps // rseg,),
        in_specs=in_specs,
        out_specs=out_specs,
        out_shape=out_shape,
        scratch_shapes=[pltpu.VMEM((tm, d), BF16)] + [pltpu.VMEM((tm, LANES), F32)] * (s5w // LANES),
        compiler_params=_cparams(("arbitrary",)),
        name="in_proj" if latent else "in_proj_ctx",
    )(*args)


def _cmul_add(ar, ai, sr, si, zr, zi):
    return ar * sr - ai * si + zr, ar * si + ai * sr + zi


def _shift_right(lo, hi, s):
    lane = lax.broadcasted_iota(jnp.int32, lo.shape, 1)
    zero = jnp.zeros_like(lo)
    if s == 0:
        return lo, hi
    if s >= LANES:
        r = s - LANES
        return zero, (lo if r == 0 else jnp.where(lane >= r, pltpu.roll(lo, r, 1), zero))
    rl, rh = pltpu.roll(lo, s, 1), pltpu.roll(hi, s, 1)
    return jnp.where(lane >= s, rl, zero), jnp.where(lane >= s, rh, rl)


def _shift_left(lo, hi, s):
    lane = lax.broadcasted_iota(jnp.int32, lo.shape, 1)
    zero = jnp.zeros_like(lo)
    if s == 0:
        return lo, hi
    if s >= LANES:
        r = s - LANES
        return (hi if r == 0 else jnp.where(lane < LANES - r, pltpu.roll(hi, LANES - r, 1), zero)), zero
    rl, rh = pltpu.roll(lo, LANES - s, 1), pltpu.roll(hi, LANES - s, 1)
    keep = lane < LANES - s
    return jnp.where(keep, rl, rh), jnp.where(keep, rh, zero)


def _cmul(a, b):
    return a[0] * b[0] - a[1] * b[1], a[0] * b[1] + a[1] * b[0]


def _split_bf16(v):
    hi = v.astype(BF16)
    return hi, (v - hi.astype(F32)).astype(BF16)


_P_BBR, _P_BBI, _P_CR, _P_CI, _P_POW_RE, _P_POW_IM = 0, 16, 32, 48, 64, 68


def _s5ops_kernel(p_ref, lt_ref, d_ref, wsum_ref, cmat_ref, mfb_ref, *, gb):
    npole2 = p_ref.shape[1]
    npole = npole2 // 2
    shape = (npole2, LANES)
    lane = lax.broadcasted_iota(jnp.int32, shape, 1)
    fwd_rows = lax.broadcasted_iota(jnp.int32, shape, 0) < npole
    slot = lane // S5_H
    bits = [((slot >> b) & 1) == 1 for b in range(3)]
    one = jnp.ones(shape, F32)
    zero = jnp.zeros(shape, F32)
    sub16 = lax.broadcasted_iota(jnp.int32, (S5_H, LANES), 0)
    lane16 = lax.broadcasted_iota(jnp.int32, (S5_H, LANES), 1)

    def tile16(v):
        v = jnp.where(lax.broadcasted_iota(jnp.int32, v.shape, 1) < S5_H, v, 0.0)
        for sh in (S5_H, 2 * S5_H, 4 * S5_H):
            v = v + pltpu.roll(v, sh, 1)
        return v

    nfield = _P_POW_RE // S5_H
    npow = 2 * (_P_POW_IM - _P_POW_RE)
    erow = lax.broadcasted_iota(jnp.int32, (LANES, (nfield + npow) * LANES), 0)
    ecol = lax.broadcasted_iota(jnp.int32, (LANES, (nfield + npow) * LANES), 1)
    eblk = ecol // LANES
    src_row = jnp.where(eblk < nfield, eblk * S5_H + ecol % S5_H, _P_POW_RE + eblk - nfield)
    spread = jnp.where(erow == src_row, 1.0, 0.0).astype(BF16)

    for q in range(gb):
        x_hi, x_lo = _split_bf16(p_ref[q])
        ex = _dot(x_hi, spread) + _dot(x_lo, spread)
        blk = lambda n: ex[:, n * LANES:(n + 1) * LANES]
        npw = _P_POW_IM - _P_POW_RE
        lam = [(blk(nfield + b), blk(nfield + npw + b)) for b in range(npw)]
        bb = blk(_P_BBR // S5_H), blk(_P_BBI // S5_H)
        cc = blk(_P_CR // S5_H), blk(_P_CI // S5_H)

        def low_power(masks):
            acc = None
            for b in range(3):
                f = jnp.where(masks[b], lam[b][0], one), jnp.where(masks[b], lam[b][1], zero)
                acc = f if acc is None else _cmul(acc, f)
            return acc

        up = low_power(bits)
        down = low_power([jnp.logical_not(m) for m in bits])
        e_tau = [up, _cmul(up, lam[3])]
        e_rev = [_cmul(down, lam[3]), down]
        e_tau1 = [_cmul(e, lam[0]) for e in e_tau]
        e_rev1 = [_cmul(e, lam[0]) for e in e_rev]

        def pick(f, b):
            return jnp.where(fwd_rows, f[0], b[0]), jnp.where(fwd_rows, f[1], b[1])

        rk = []
        for hf in range(2):
            cols = slice(hf * LANES, (hf + 1) * LANES)
            sr, si = _cmul(pick(e_rev[hf], e_tau[hf]), bb)
            wsum_ref[q, 0:npole2, cols] = sr.astype(BF16)
            wsum_ref[q, npole2:, cols] = si.astype(BF16)
            rr, ri = _cmul(pick(e_tau1[hf], e_rev1[hf]), cc)
            cmat_ref[q, 0:npole2, cols] = rr.astype(BF16)
            cmat_ref[q, npole2:, cols] = (-ri).astype(BF16)
            rk.append(_cmul(pick(e_tau[hf], e_rev[hf]), cc))

        kts = []
        for dr in range(2):
            rows = slice(dr * npole, (dr + 1) * npole)
            rhs = jnp.concatenate([jnp.concatenate([rk[0][part][rows], rk[1][part][rows]], axis=1)
                                   for part in range(2)], axis=0)
            l_hi, l_lo = _split_bf16(lt_ref[q, dr * S5_H:(dr + 1) * S5_H, :])
            r_hi, r_lo = _split_bf16(rhs)
            kts.append(_dot(l_hi, r_hi) + _dot(l_lo, r_hi) + _dot(l_hi, r_lo))
        kf_lo, kf_hi = kts[0][:, 0:LANES], kts[0][:, LANES:]
        kb_lo, kb_hi = kts[1][:, 0:LANES], kts[1][:, LANES:]

        dsk = tile16(jnp.broadcast_to(d_ref[q, 0:1, :], (S5_H, LANES)))
        for ti in range(T_CHUNK):
            f_lo, f_hi = _shift_right(kf_lo, kf_hi, ti * S5_H)
            b_lo, b_hi = _shift_left(kb_lo, kb_hi, (T_CHUNK - 1 - ti) * S5_H)
            diag = jnp.where(lane16 == (ti % 8) * S5_H + sub16, dsk, 0.0)
            lo = f_lo + b_lo + (diag if ti < 8 else 0.0)
            hi = f_hi + b_hi + (diag if ti >= 8 else 0.0)
            mfb_ref[q, ti * S5_H:(ti + 1) * S5_H, :] = jnp.concatenate([lo, hi], axis=1).astype(BF16)


def _s5ops(pblk, lt, dvec, *, gb):
    ngroups, npole2, _ = pblk.shape
    kc = T_CHUNK * S5_H
    blk = lambda a: pl.BlockSpec((gb,) + a.shape[1:], lambda i: (i,) + (0,) * (a.ndim - 1))
    out = jax.ShapeDtypeStruct((ngroups, 2 * npole2, kc), BF16)
    return pl.pallas_call(
        functools.partial(_s5ops_kernel, gb=gb),
        grid=(ngroups // gb,),
        in_specs=[blk(pblk), blk(lt), blk(dvec)],
        out_specs=[blk(out), blk(out), blk(out)],
        out_shape=[out, out, out],
        compiler_params=_cparams(("arbitrary",)),
        name="s5_ops",
    )(pblk, lt, dvec)


def _s5core_kernel(u2_ref, u2c_ref, wsum_ref, mfb_ref, cmat_ref, a_ref, y2_ref, z_s, s_s, car_s, *, gb, cps, ncc):
    half = LANES // 2
    isf8 = lax.broadcasted_iota(jnp.int32, (SUBLANES, LANES), 1) < half
    isf1 = lax.broadcasted_iota(jnp.int32, (1, LANES), 1) < half

    for q in range(gb):
        z_s[q] = _dot_nt(u2_ref[q], wsum_ref[q])

    ctx_fin = []
    for q in range(gb):
        zc = _dot_nt(u2c_ref[q], wsum_ref[q])
        ar, ai = a_ref[q, 0:1, :], a_ref[q, 1:2, :]
        sr = jnp.zeros((1, LANES), F32)
        si = jnp.zeros((1, LANES), F32)
        for k in range(ncc):
            kb = ncc - 1 - k
            zr = jnp.where(isf1, zc[k:k + 1, 0:LANES], zc[kb:kb + 1, 0:LANES])
            zi = jnp.where(isf1, zc[k:k + 1, LANES:], zc[kb:kb + 1, LANES:])
            sr, si = _cmul_add(ar, ai, sr, si, zr, zi)
        ctx_fin.append((sr, si))

    def body1(k, carry):
        rf = pl.multiple_of(k * SUBLANES, SUBLANES)
        rb = pl.multiple_of((cps - 1 - k) * SUBLANES, SUBLANES)
        new = []
        for q in range(gb):
            sr, si = carry[q]
            s_s[q, pl.ds(rf, SUBLANES), 0:half] = sr[:, 0:half]
            s_s[q, pl.ds(rb, SUBLANES), half:LANES] = sr[:, half:]
            s_s[q, pl.ds(rf, SUBLANES), LANES:LANES + half] = si[:, 0:half]
            s_s[q, pl.ds(rb, SUBLANES), LANES + half:] = si[:, half:]
            zr = jnp.where(isf8, z_s[q, pl.ds(rf, SUBLANES), 0:LANES], z_s[q, pl.ds(rb, SUBLANES), 0:LANES])
            zi = jnp.where(isf8, z_s[q, pl.ds(rf, SUBLANES), LANES:], z_s[q, pl.ds(rb, SUBLANES), LANES:])
            new.append(_cmul_add(a_ref[q, 0:1, :], a_ref[q, 1:2, :], sr, si, zr, zi))
        return tuple(new)

    zero8 = jnp.zeros((SUBLANES, LANES), F32)
    fin = lax.fori_loop(0, cps, body1, tuple((zero8, zero8) for _ in range(gb)))

    for q in range(gb):
        fr, fi = fin[q]
        asr, asi = a_ref[q, 2:3, :], a_ref[q, 3:4, :]
        cr, ci = ctx_fin[q]
        car_s[q, 0, 0:1, 0:half] = cr[:, 0:half]
        car_s[q, 0, SEG - 1:SEG, half:] = cr[:, half:]
        car_s[q, 1, 0:1, 0:half] = ci[:, 0:half]
        car_s[q, 1, SEG - 1:SEG, half:] = ci[:, half:]
        for m in range(SEG - 1):
            mb = SEG - 1 - m
            zr = jnp.where(isf1, fr[m:m + 1, :], fr[mb:mb + 1, :])
            zi = jnp.where(isf1, fi[m:m + 1, :], fi[mb:mb + 1, :])
            cr, ci = _cmul_add(asr, asi, cr, ci, zr, zi)
            car_s[q, 0, m + 1:m + 2, 0:half] = cr[:, 0:half]
            car_s[q, 0, mb - 1:mb, half:] = cr[:, half:]
            car_s[q, 1, m + 1:m + 2, 0:half] = ci[:, 0:half]
            car_s[q, 1, mb - 1:mb, half:] = ci[:, half:]

    def body2(k, pw):
        rf = pl.multiple_of(k * SUBLANES, SUBLANES)
        rb = pl.multiple_of((cps - 1 - k) * SUBLANES, SUBLANES)
        new = []
        for q in range(gb):
            pr, pi = pw[q]
            cr, ci = car_s[q, 0], car_s[q, 1]
            tr = pr * cr - pi * ci
            ti = pr * ci + pi * cr
            fwd = s_s[q, pl.ds(rf, SUBLANES), :]
            bwd = s_s[q, pl.ds(rb, SUBLANES), :]
            z_s[q, pl.ds(rf, SUBLANES), 0:half] = fwd[:, 0:half] + tr[:, 0:half]
            z_s[q, pl.ds(rb, SUBLANES), half:LANES] = bwd[:, half:LANES] + tr[:, half:]
            z_s[q, pl.ds(rf, SUBLANES), LANES:LANES + half] = fwd[:, LANES:LANES + half] + ti[:, 0:half]
            z_s[q, pl.ds(rb, SUBLANES), LANES + half:] = bwd[:, LANES + half:] + ti[:, half:]
            ar, ai = a_ref[q, 0:1, :], a_ref[q, 1:2, :]
            new.append((pr * ar - pi * ai, pr * ai + pi * ar))
        return tuple(new)

    one8 = jnp.ones((SUBLANES, LANES), F32)
    lax.fori_loop(0, cps, body2, tuple((one8, zero8) for _ in range(gb)))

    for q in range(gb):
        y2_ref[q] = _dot(u2_ref[q], mfb_ref[q]) + _dot(z_s[q].astype(BF16), cmat_ref[q])


def _s5core(u2, u2c, wsum_t, mfb, cmat, avec, *, gb, cps):
    ngroups, nrows, kc = u2.shape
    ncc = u2c.shape[1]
    ns = wsum_t.shape[1]
    blk = lambda a: pl.BlockSpec((gb,) + a.shape[1:], lambda i: (i,) + (0,) * (a.ndim - 1))
    return pl.pallas_call(
        functools.partial(_s5core_kernel, gb=gb, cps=cps, ncc=ncc),
        grid=(ngroups // gb,),
        in_specs=[blk(u2), blk(u2c), blk(wsum_t), blk(mfb), blk(cmat), blk(avec)],
        out_specs=pl.BlockSpec((gb, nrows, kc), lambda i: (i, 0, 0)),
        out_shape=jax.ShapeDtypeStruct((ngroups, nrows, kc), F32),
        scratch_shapes=[pltpu.VMEM((gb, nrows, ns), F32), pltpu.VMEM((gb, nrows, ns), F32),
                        pltpu.VMEM((gb, 2, SUBLANES, LANES), F32)],
        compiler_params=_cparams(("arbitrary",)),
        name="s5_core",
    )(u2, u2c, wsum_t, mfb, cmat, avec)


CONV_HALO = 16


def _mixer0_stage(i, last, x_ref, rt_ref, ct_ref, y2_ref, vb_ref, vbp_ref, vbn_ref, wglu_ref, cw_ref, cb_ref,
                  lng_ref, lnb_ref, wout_ref, g1_ref, h_dst, ext_s, cat_s, ys_s):
    nseg, rseg, d = x_ref.shape
    tm = nseg * rseg
    s5w = len(ys_s) * LANES
    cwid = vb_ref.shape[2]
    nblk = rseg // T_CHUNK
    yield (len(ys_s) * nblk + tm // ROW_BLOCK + nseg // 2) * 3 // 4
    yield from _from_chunked(y2_ref, ys_s, nblk, nseg, rseg)

    wglu = wglu_ref[...].astype(BF16)
    for r in range(tm // ROW_BLOCK):
        rows = slice(r * ROW_BLOCK, (r + 1) * ROW_BLOCK)
        ya = jax.nn.gelu(jnp.concatenate([ys_ref[rows, :] for ys_ref in ys_s], axis=1))
        ya = ya * jax.nn.sigmoid(_dot(ya.astype(BF16), wglu))
        cat_s[rows, 0:s5w] = ya.astype(BF16)
        yield

    zeros_h = jnp.zeros((CONV_HALO, cwid), F32)
    pad = CONV_K // 2
    ext_rows = rseg + 2 * CONV_HALO
    nacc = rseg + SUBLANES
    for seg in range(nseg):
        ext = ext_s.at[seg % ext_s.shape[0]]
        prev_first = vbp_ref[seg - 1] if seg > 0 else zeros_h
        next_last = vbn_ref[seg + 1] if seg < nseg - 1 else zeros_h
        ext[0:CONV_HALO, :] = jnp.where(i > 0, vbp_ref[seg], prev_first)
        ext[CONV_HALO:CONV_HALO + rseg, :] = vb_ref[seg]
        ext[CONV_HALO + rseg:ext_rows, :] = jnp.where(i < last, vbn_ref[seg], next_last)
        acc = jnp.broadcast_to(cb_ref[...], (rseg, cwid))
        for s in range(SUBLANES):
            part = None
            for qq in range(ext_rows // SUBLANES):
                k = qq * SUBLANES + s - (CONV_HALO - pad)
                if k < 0 or k >= CONV_K or qq * SUBLANES + nacc > ext_rows:
                    continue
                term = cw_ref[k:k + 1, :] * ext[qq * SUBLANES:qq * SUBLANES + nacc, :]
                part = term if part is None else part + term
            acc = acc + part[s:s + rseg, :]
        mu = jnp.mean(acc, axis=-1, keepdims=True)
        xc = acc - mu
        var = jnp.mean(xc * xc, axis=-1, keepdims=True)
        y = xc * lax.rsqrt(var + EPS) * lng_ref[...] + lnb_ref[...]
        y = y * jax.nn.sigmoid(y)
        cat_s[seg * rseg:(seg + 1) * rseg, s5w:] = y.astype(BF16)
        if seg % 2 == 1 and seg < nseg - 1:
            yield

    yield TAIL
    half = ct_ref.shape[1]
    for j in range(d // MXU_COLS):
        c0 = j * MXU_COLS
        cols = slice(c0, c0 + MXU_COLS)
        yx = _dot(cat_s[...], wout_ref[:, cols].astype(BF16))
        g1 = g1_ref[:, cols]
        for seg in range(nseg):
            rows = slice(seg * rseg, (seg + 1) * rseg)
            if c0 < half:
                pos = jnp.broadcast_to(rt_ref[seg:seg + 1, cols], (rseg, MXU_COLS))
            else:
                pos = ct_ref[:, c0 - half:c0 - half + MXU_COLS]
            h_dst[rows, cols] = x_ref[seg, :, cols] + pos + g1 * yx[rows, :]
        yield


FF_CHUNK = 1024


MXU_COLS = 256
ROW_BLOCK = 128


def _mlp_stage(h_src, g_ref, sh_ref, sc_ref, g2_ref, w1_ref, w2_ref, fg_ref, final, store, hc_s, nb_s, a_s, acc_s):
    tm, d = hc_s.shape
    dff = w1_ref.shape[1]
    nrb, nch, nsub = tm // ROW_BLOCK, dff // FF_CHUNK, FF_CHUNK // MXU_COLS
    yield 2 * nrb + nch * (nsub + d // MXU_COLS)
    gm = g_ref[...] * (1.0 + sc_ref[...])
    for r in range(nrb):
        rows = slice(r * ROW_BLOCK, (r + 1) * ROW_BLOCK)
        h = h_src[rows, :]
        hc_s[rows, :] = h
        nb_s[rows, :] = _rms_mod(h, gm, sh_ref[...]).astype(BF16)
        yield
    for c in range(nch):
        a_buf = a_s[c % len(a_s)]
        for j in range(nsub):
            col0 = c * FF_CHUNK + j * MXU_COLS
            a = jnp.maximum(_dot(nb_s[...], w1_ref[:, col0:col0 + MXU_COLS]), 0.0)
            a_buf[:, j * MXU_COLS:(j + 1) * MXU_COLS] = (a * a).astype(BF16)
            yield
        for j in range(d // MXU_COLS):
            cols = slice(j * MXU_COLS, (j + 1) * MXU_COLS)
            p = _dot(a_buf[...], w2_ref[c * FF_CHUNK:(c + 1) * FF_CHUNK, cols])
            if c == 0:
                acc_s[:, cols] = p
            else:
                acc_s[:, cols] += p
            yield
    for r in range(nrb):
        rows = slice(r * ROW_BLOCK, (r + 1) * ROW_BLOCK)
        out = hc_s[rows, :] + g2_ref[...] * acc_s[rows, :]
        if final:
            ms = jnp.mean(out * out, axis=-1, keepdims=True)
            out = out * lax.rsqrt(ms + EPS) * fg_ref[...]
        store(r, out)
        yield


def _mlp_scratch(tm, d):
    return [pltpu.VMEM((tm, d), F32), pltpu.VMEM((tm, d), BF16), pltpu.VMEM((tm, FF_CHUNK), BF16),
            pltpu.VMEM((tm, FF_CHUNK), BF16), pltpu.VMEM((tm, d), F32)]


TAIL = "tail"


def _interleave(*stages):
    counts = [next(st) for st in stages]
    done = [0] * len(stages)
    live = set(range(len(stages)))
    deferred = []
    while live:
        k = min(live, key=lambda s: (done[s] + 0.5) / counts[s])
        try:
            if next(stages[k]) == TAIL:
                live.remove(k)
                deferred.append(k)
            done[k] += 1
        except StopIteration:
            live.remove(k)
    for k in deferred:
        for _ in stages[k]:
            pass


def _layer0_kernel(x_ref, rt_ref, ct_ref, y2_ref, vb_ref, vbp_ref, vbn_ref, wglu_ref, cw_ref, cb_ref, lng_ref,
                   lnb_ref, wout_ref, g1_ref, g_ref, sh_ref, sc_ref, g2_ref, w1_ref, w2_ref, w1n_ref, w2n_ref,
                   o_ref, w1nb_ref, w2nb_ref, h_s, ext_s, cat_s, hc_s, nb_s, a0_s, a1_s, acc_s, *ys_s):
    step = pl.program_id(0)
    last = pl.num_programs(0) - 2
    i = jnp.minimum(step, last)
    nseg, rseg, _ = x_ref.shape
    w1nb_ref[...] = w1n_ref[...].astype(BF16)
    w2nb_ref[...] = w2n_ref[...].astype(BF16)

    @pl.when(step == 0)
    def _():
        h_s[...] = jnp.zeros(h_s.shape, F32)

    def store(r, out):
        for k in range(ROW_BLOCK // rseg):
            o_ref[r * (ROW_BLOCK // rseg) + k] = out[k * rseg:(k + 1) * rseg, :]

    _interleave(
        _mlp_stage(h_s, g_ref, sh_ref, sc_ref, g2_ref, w1_ref, w2_ref, None, False, store,
                   hc_s, nb_s, (a0_s, a1_s), acc_s),
        _mixer0_stage(i, last, x_ref, rt_ref, ct_ref, y2_ref, vb_ref, vbp_ref, vbn_ref, wglu_ref, cw_ref, cb_ref,
                      lng_ref, lnb_ref, wout_ref, g1_ref, h_s, ext_s, cat_s, ys_s))


def _layer0(x3, tabs, y2, vb3, wglu, conv_w, conv_b, ln_g, ln_b, wout, g1, g, sh, sc, g2, w1, w2, next_mlp_w,
            *, next_layer):
    nseg, rps, d = x3.shape
    ngroups, _, kc = y2.shape
    s5w = ngroups * S5_H
    cw = vb3.shape[2]
    rseg = SLAB
    tm = nseg * rseg
    nt = rps // rseg
    rt, ct = tabs
    hb = rseg // CONV_HALO
    nhb = rps // CONV_HALO
    cur = lambda s: jnp.minimum(s, nt - 1)
    vec = lambda a: pl.BlockSpec(a.shape, lambda s: (0,) * a.ndim)
    tile = lambda w: pl.BlockSpec((nseg, rseg, w), lambda s: (0, cur(s), 0))
    casts = [_cast_specs(w, nt, next_layer, cur) for w in next_mlp_w]
    return pl.pallas_call(
        _layer0_kernel,
        grid=(nt + 1,),
        in_specs=[
            tile(d),
            pl.BlockSpec((None, nseg, rt.shape[2]), lambda s: (cur(s), 0, 0)),
            _resident(ct),
            pl.BlockSpec((ngroups, tm // T_CHUNK, kc), lambda s: (0, cur(s), 0)),
            tile(cw),
            pl.BlockSpec((nseg, CONV_HALO, cw), lambda s: (0, (cur(s) * hb + nhb - 1) % nhb, 0)),
            pl.BlockSpec((nseg, CONV_HALO, cw), lambda s: (0, ((cur(s) + 1) * hb) % nhb, 0)),
            _resident(wglu), vec(conv_w), vec(conv_b), vec(ln_g), vec(ln_b), _resident(wout), vec(g1),
            vec(g), vec(sh), vec(sc), vec(g2), _resident(w1), _resident(w2), casts[0][0], casts[1][0],
        ],
        out_specs=[pl.BlockSpec((nseg, rseg, d), lambda s: (0, jnp.maximum(s - 1, 0), 0)),
                   casts[0][1], casts[1][1]],
        out_shape=[jax.ShapeDtypeStruct((nseg, rps, d), F32), casts[0][2], casts[1][2]],
        scratch_shapes=[pltpu.VMEM((tm, d), F32), pltpu.VMEM((2, rseg + 2 * CONV_HALO, cw), F32),
                        pltpu.VMEM((tm, d), BF16)] + _mlp_scratch(tm, d)
        + [pltpu.VMEM((tm, LANES), F32)] * (s5w // LANES),
        compiler_params=_cparams(("arbitrary",)),
        name="layer0_tail",
    )(x3, rt, ct, y2, vb3, vb3, vb3, wglu, conv_w, conv_b, ln_g, ln_b, wout, g1, g, sh, sc, g2, w1, w2,
      *next_mlp_w)


POOL_HALO = 8


def _pool_stage(i, last, h_ref, hp_ref, hn_ref, g_ref, sh_ref, sc_ref, g1_ref, pw_ref, pb_ref, ps_ref, h_dst, ext_s,
                tm, seq_len):
    nrb = tm // ROW_BLOCK
    yield 1 + nrb + len(POOL_WINDOWS) * nrb
    gm = g_ref[...] * (1.0 + sc_ref[...])
    sh = sh_ref[...]
    ext_s[0:POOL_HALO, :] = jnp.where(i > 0, _rms_mod(hp_ref[...], gm, sh), 0.0)
    ext_s[POOL_HALO + tm:, :] = jnp.where(i < last, _rms_mod(hn_ref[...], gm, sh), 0.0)
    yield
    for r in range(nrb):
        rows = slice(r * ROW_BLOCK, (r + 1) * ROW_BLOCK)
        ext_s[POOL_HALO + r * ROW_BLOCK:POOL_HALO + (r + 1) * ROW_BLOCK, :] = _rms_mod(h_ref[rows, :], gm, sh)
        yield
    pc = pw_ref.shape[1]
    for gi, win in enumerate(POOL_WINDOWS):
        cols = slice(gi * pc, (gi + 1) * pc)
        left = win // 2
        right = win - 1 - left
        wgt = pw_ref[gi].astype(BF16)
        for r in range(nrb):
            rows = slice(r * ROW_BLOCK, (r + 1) * ROW_BLOCK)
            row0 = POOL_HALO + r * ROW_BLOCK
            s = None
            for off in range(-left, right + 1):
                v = ext_s[row0 + off:row0 + off + ROW_BLOCK, cols]
                s = v if s is None else s + v
            t = i * tm + r * ROW_BLOCK + lax.broadcasted_iota(jnp.int32, (ROW_BLOCK, 1), 0)
            lo = jnp.maximum(t - left, 0)
            hi = jnp.minimum(t + right, seq_len - 1)
            cnt = (hi - lo + 1).astype(F32)
            pg = s / cnt - ext_s[row0:row0 + ROW_BLOCK, cols]
            y = (_dot(pg.astype(BF16), wgt) + pb_ref[:, cols]) * ps_ref[:, cols]
            h_dst[rows, cols] = h_ref[rows, cols] + g1_ref[:, cols] * y
            yield


def _layer1_kernel(h_ref, hp_ref, hn_ref, gx_ref, shx_ref, scx_ref, g1_ref, pw_ref, pb_ref, ps_ref,
                   g_ref, sh_ref, sc_ref, g2_ref, w1_ref, w2_ref, fg_ref, o_ref, h_s, ext_s,
                   hc_s, nb_s, a0_s, a1_s, acc_s, *, tm, seq_len):
    step = pl.program_id(0)
    last = pl.num_programs(0) - 2
    i = jnp.minimum(step, last)

    @pl.when(step == 0)
    def _():
        h_s[...] = jnp.zeros(h_s.shape, F32)

    def store(r, out):
        o_ref[r * ROW_BLOCK:(r + 1) * ROW_BLOCK, :] = out

    _interleave(
        _mlp_stage(h_s, g_ref, sh_ref, sc_ref, g2_ref, w1_ref, w2_ref, fg_ref, True, store,
                   hc_s, nb_s, (a0_s, a1_s), acc_s),
        _pool_stage(i, last, h_ref, hp_ref, hn_ref, gx_ref, shx_ref, scx_ref, g1_ref, pw_ref, pb_ref, ps_ref,
                    h_s, ext_s, tm, seq_len))


def _layer1(h, gx, shx, scx, g1, pool_w, pool_b, pool_scale, g, sh, sc, g2, w1, w2, fg, *, tm):
    rows, d = h.shape
    nt = rows // tm
    hb = tm // POOL_HALO
    nhb = rows // POOL_HALO
    cur = lambda s: jnp.minimum(s, nt - 1)
    vec = pl.BlockSpec((1, d), lambda s: (0, 0))
    return pl.pallas_call(
        functools.partial(_layer1_kernel, tm=tm, seq_len=rows),
        grid=(nt + 1,),
        in_specs=[
            pl.BlockSpec((tm, d), lambda s: (cur(s), 0)),
            pl.BlockSpec((POOL_HALO, d), lambda s: (jnp.maximum(cur(s) * hb - 1, 0), 0)),
            pl.BlockSpec((POOL_HALO, d), lambda s: (jnp.minimum((cur(s) + 1) * hb, nhb - 1), 0)),
            vec, vec, vec, vec, _resident(pool_w), vec, vec,
            vec, vec, vec, vec, _resident(w1), _resident(w2), vec,
        ],
        out_specs=pl.BlockSpec((tm, d), lambda s: (jnp.maximum(s - 1, 0), 0)),
        out_shape=jax.ShapeDtypeStruct((rows, d), F32),
        scratch_shapes=[pltpu.VMEM((tm, d), F32), pltpu.VMEM((tm + 2 * POOL_HALO, d), F32)] + _mlp_scratch(tm, d),
        compiler_params=_cparams(("arbitrary",)),
        name="layer1",
    )(h, h, h, gx, shx, scx, g1, pool_w, pool_b, pool_scale, g, sh, sc, g2, w1, w2, fg)


def _pos_tables(rows, d, nseg):
    quarter = d // 4
    omega = 1.0 / (10000.0 ** (jnp.arange(quarter, dtype=F32) / quarter))

    def emb(n):
        ang = jnp.arange(n, dtype=F32)[:, None] * omega[None, :]
        return jnp.concatenate([jnp.sin(ang), jnp.cos(ang)], axis=-1)

    rt = jnp.swapaxes(emb(rows).reshape(nseg, rows // nseg, 2 * quarter), 0, 1)
    return rt, emb(GRID_W)


def _s5_params(lam_re, lam_im, log_step, b_re, b_im, c_re, c_im, d_skip, cps):
    ngroups, npole = lam_re.shape[1:]
    t = T_CHUNK
    lr = jnp.minimum(lam_re.astype(F32), LAMBDA_RE_MAX)
    li = lam_im.astype(F32)
    dt = jnp.exp(log_step.astype(F32))[..., None]
    ldr, ldi = lr * dt, li * dt

    def cpow(n):
        mag = jnp.exp(ldr * n)
        return mag * jnp.cos(ldi * n), mag * jnp.sin(ldi * n)

    lbr, lbi = cpow(1.0)
    den = lr * lr + li * li
    qr = ((lbr - 1.0) * lr + lbi * li) / den
    qi = (lbi * lr - (lbr - 1.0) * li) / den
    br, bi = b_re.astype(F32), b_im.astype(F32)
    bbr = qr[..., None] * br - qi[..., None] * bi
    bbi = qr[..., None] * bi + qi[..., None] * br
    cr = jnp.swapaxes(c_re.astype(F32), 2, 3)
    ci = jnp.swapaxes(c_im.astype(F32), 2, 3)

    pows = [cpow(float(2 ** b)) for b in range(4)]
    pow_re = jnp.stack([p[0] for p in pows], axis=-1)
    pow_im = jnp.stack([p[1] for p in pows], axis=-1)
    fill = jnp.zeros(bbr.shape[:3] + (LANES - 4 * S5_H - 8,), F32)
    pblk = jnp.concatenate([bbr, bbi, cr, ci, pow_re, pow_im, fill], axis=-1)
    pblk = jnp.moveaxis(pblk, 0, 1).reshape(ngroups, 2 * npole, LANES)

    lt = jnp.concatenate([jnp.swapaxes(bbr, 2, 3), -jnp.swapaxes(bbi, 2, 3)], axis=-1)
    lt = jnp.moveaxis(lt, 0, 1).reshape(ngroups, 2 * S5_H, 2 * npole)

    dvec = jnp.zeros((ngroups, SUBLANES, LANES), F32).at[:, 0, :S5_H].set(d_skip.astype(F32).reshape(ngroups, S5_H))

    atr, ati = cpow(float(t))
    asr, asi = cpow(float(t * cps))
    cat2 = lambda v: jnp.concatenate([v[0], v[1]], axis=-1)
    avec = jnp.stack([cat2(atr), cat2(ati), cat2(asr), cat2(asi)], axis=1)
    avec = jnp.concatenate([avec, jnp.zeros((ngroups, SUBLANES - 4, 2 * npole), F32)], axis=1)
    return pblk, lt, dvec, avec


def kernel(x, c, ctx, c_ctx, w_ada, b_ada, norm_mix_g, norm_mlp_g, w_in, w_out, s5_lam_re, s5_lam_im, s5_log_step, s5_b_re, s5_b_im, s5_c_re, s5_c_im, s5_d, s5_w_glu, conv_w, conv_b, conv_ln_g, conv_ln_b, pool_w, pool_b, pool_scale, mlp_w1, mlp_w2, final_g):
    bsz, seq, d = x.shape
    assert bsz == 1 and c.shape[0] == 1 and ctx.shape[0] == 1
    assert w_ada.shape[0] == 2, "the kernels implement the depth-2 block (one even layer, one odd layer)"
    s5w = s5_d.shape[1]
    rps = seq // SEG
    cps = rps // T_CHUNK
    tm = 512

    x3 = x.reshape(SEG, rps, d)
    tabs = _pos_tables(seq // GRID_W, d, SEG)

    cc = jnp.zeros((16, d), F32).at[0].set(c[0].astype(F32)).at[1].set(c_ctx.astype(F32))
    mods = _ada(cc, w_ada, b_ada)

    def mod(layer, row, which):
        return lax.slice(mods[layer], (row, which * d), (row + 1, (which + 1) * d))

    row_vec = lambda v: v.reshape(1, -1).astype(F32)

    pblk, lt, dvec, avec = _s5_params(s5_lam_re[0], s5_lam_im[0], s5_log_step[0], s5_b_re[0], s5_b_im[0],
                                      s5_c_re[0], s5_c_im[0], s5_d[0], cps)
    wsum_t, cmat, mfb = _s5ops(pblk, lt, dvec, gb=8)
    g_mix0 = row_vec(norm_mix_g[0])
    u2, vb3, w1_l0, w2_l0 = _inproj(x3, tabs, g_mix0, mod(0, 0, 0), mod(0, 0, 1), w_in[0], (mlp_w1, mlp_w2), s5w=s5w)
    u2c, = _inproj(ctx, None, g_mix0, mod(0, 1, 0), mod(0, 1, 1), w_in[0], s5w=s5w)
    y2 = _s5core(u2, u2c, wsum_t, mfb, cmat, avec, gb=4, cps=cps)
    h3, w1_l1, w2_l1 = _layer0(x3, tabs, y2, vb3, s5_w_glu[0], conv_w[0].astype(F32), row_vec(conv_b[0]),
                               row_vec(conv_ln_g[0]), row_vec(conv_ln_b[0]), w_out[0], mod(0, 0, 2),
                               row_vec(norm_mlp_g[0]), mod(0, 0, 3), mod(0, 0, 4), mod(0, 0, 5), w1_l0, w2_l0,
                               (mlp_w1, mlp_w2), next_layer=1)
    h = h3.reshape(seq, d)

    h = _layer1(h, row_vec(norm_mix_g[1]), mod(1, 0, 0), mod(1, 0, 1), mod(1, 0, 2),
                pool_w[0], row_vec(pool_b[0]), row_vec(pool_scale[0]),
                row_vec(norm_mlp_g[1]), mod(1, 0, 3), mod(1, 0, 4), mod(1, 0, 5), w1_l1, w2_l1,
                row_vec(final_g), tm=tm)
    return h[None]
```
